```python
import jax, jax.numpy as jnp
from jax import lax
import numpy as np

D_MODEL = 1024
BATCH = 16
SEQ = 2048
DEPTH = 2
DEC_BATCH = 8
DEC_SEQ = 32
PAST_LEN = 4096

CHUNK = 64
MIX_WIDTH = D_MODEL
C_CONV = MIX_WIDTH // 2
CONV_WIDTH = 31
N_HEADS = 8
HEAD_DIM = (MIX_WIDTH - C_CONV) // N_HEADS
ATT_WIDTH = N_HEADS * HEAD_DIM
N_IDX_HEADS = 8
IDX_DIM = 64
MAX_TOPK = 256
Q_BLOCK = 128
D_FF = 2816
ROPE_THETA = 10000.0
EPS = 1e-6
ATT_SCALE = HEAD_DIM ** -0.5
IDX_SCALE = IDX_DIM ** -0.5
IDX_HEAD_SCALE = N_IDX_HEADS ** -0.5
IN_COLS = 2 * C_CONV + 3 * ATT_WIDTH + N_IDX_HEADS * IDX_DIM + IDX_DIM + N_IDX_HEADS

kernel_name = 'hymba_conformer_dsa_stream_step'


def rms_norm(x, g):
    xf = x.astype(jnp.float32)
    y = xf * lax.rsqrt(jnp.mean(xf * xf, axis=-1, keepdims=True) + EPS)
    return (y * g.astype(jnp.float32)).astype(x.dtype)


def layer_norm(x, g, b):
    xf = x.astype(jnp.float32)
    mu = jnp.mean(xf, axis=-1, keepdims=True)
    var = jnp.mean(jnp.square(xf - mu), axis=-1, keepdims=True)
    y = (xf - mu) * lax.rsqrt(var + EPS) * g.astype(jnp.float32) + b.astype(jnp.float32)
    return y.astype(x.dtype)


def rope_tables(pos, dim):
    inv = 1.0 / (ROPE_THETA ** (jnp.arange(0, dim, 2, dtype=jnp.float32) / dim))
    ang = pos.astype(jnp.float32)[:, None] * inv[None, :]
    return jnp.cos(ang), jnp.sin(ang)


def apply_rope(x, cos, sin):
    half = x.shape[-1] // 2
    shp = (cos.shape[0],) + (1,) * (x.ndim - 3) + (half,)
    c, s = cos.reshape(shp), sin.reshape(shp)
    x1 = x[..., :half].astype(jnp.float32)
    x2 = x[..., half:].astype(jnp.float32)
    return jnp.concatenate([x1 * c - x2 * s, x2 * c + x1 * s], axis=-1).astype(x.dtype)


def half_step_ffn(x, g_pre, g_post, w_up, w_down):
    h = rms_norm(x, g_pre) @ w_up
    gate, up = h[..., :D_FF], h[..., D_FF:]
    y = (jax.nn.silu(gate) * up) @ w_down
    return x + 0.5 * rms_norm(y, g_post)


def split_cols(proj):
    sizes = (C_CONV, C_CONV, ATT_WIDTH, ATT_WIDTH, ATT_WIDTH,
             N_IDX_HEADS * IDX_DIM, IDX_DIM, N_IDX_HEADS)
    out, start = [], 0
    for s in sizes:
        out.append(proj[..., start:start + s])
        start += s
    return out


def conformer_conv(a, g, hist, conv_w, conv_b, ln_g, ln_b):
    u = a * jax.nn.sigmoid(g)
    ext = jnp.concatenate([hist.astype(u.dtype), u], axis=1)
    y = lax.conv_general_dilated(
        ext, conv_w[:, None, :].astype(u.dtype), window_strides=(1,), padding='VALID',
        dimension_numbers=('NWC', 'WIO', 'NWC'), feature_group_count=C_CONV) + conv_b
    y = jax.nn.silu(layer_norm(y, ln_g, ln_b))
    return y, ext[:, ext.shape[1] - (CONV_WIDTH - 1):]


def dsa_block(q, qi, wi, q_pos, k, v, ki, k_pos, topk):
    logits = jnp.einsum('thd,sd->ths', qi, ki).astype(jnp.float32) * IDX_SCALE
    score = jnp.einsum('th,ths->ts', wi.astype(jnp.float32) * IDX_HEAD_SCALE, jax.nn.relu(logits))
    q_chunk = q_pos // CHUNK
    k_chunk = k_pos // CHUNK
    admissible = k_chunk[None, :] <= q_chunk[:, None]
    score = jnp.where(admissible, score, -jnp.inf)
    _, idx = lax.top_k(score, topk)
    valid = k_chunk[idx] <= q_chunk[:, None]
    k_sel = k[idx]
    v_sel = v[idx]
    att = jnp.einsum('thd,tkhd->thk', q, k_sel).astype(jnp.float32) * ATT_SCALE
    att = jnp.where(valid[:, None, :], att, -jnp.inf)
    p = jax.nn.softmax(att, axis=-1).astype(v.dtype)
    return jnp.einsum('thk,tkhd->thd', p, v_sel)


def sparse_attention(q, qi, wi, q_pos, k, v, ki):
    T = q.shape[1]
    S = k.shape[1]
    topk = min(MAX_TOPK, S // 4)
    qb = min(Q_BLOCK, T)
    nb = T // qb
    k_pos = jnp.arange(S)

    def per_seq(args):
        q_s, qi_s, wi_s, k_s, v_s, ki_s = args

        def per_block(blk):
            qb_, qib_, wib_, pb_ = blk
            return dsa_block(qb_, qib_, wib_, pb_, k_s, v_s, ki_s, k_pos, topk)

        blocks = (q_s.reshape(nb, qb, N_HEADS, HEAD_DIM),
                  qi_s.reshape(nb, qb, N_IDX_HEADS, IDX_DIM),
                  wi_s.reshape(nb, qb, N_IDX_HEADS),
                  q_pos.reshape(nb, qb))
        return lax.map(per_block, blocks).reshape(T, N_HEADS, HEAD_DIM)

    return lax.map(per_seq, (q, qi, wi, k, v, ki))


def hybrid_mixer(h, pos, past, w_in, conv_w, conv_b, ln_g, ln_b, w_out):
    B, T, _ = h.shape
    a, g, q, k, v, qi, ki, wi = split_cols(h @ w_in)
    q = q.reshape(B, T, N_HEADS, HEAD_DIM)
    k = k.reshape(B, T, N_HEADS, HEAD_DIM)
    v = v.reshape(B, T, N_HEADS, HEAD_DIM)
    qi = qi.reshape(B, T, N_IDX_HEADS, IDX_DIM)
    cos, sin = rope_tables(pos, HEAD_DIM)
    cos_i, sin_i = rope_tables(pos, IDX_DIM)
    q = apply_rope(q, cos, sin)
    k = apply_rope(k, cos, sin)
    qi = apply_rope(qi, cos_i, sin_i)
    ki = apply_rope(ki, cos_i, sin_i)
    if past is None:
        k_all, v_all, ki_all = k, v, ki
        conv_hist = jnp.zeros((B, CONV_WIDTH - 1, C_CONV), h.dtype)
    else:
        k_past, v_past, ki_past, conv_hist = past
        k_all = jnp.concatenate([k_past.astype(k.dtype), k], axis=1)
        v_all = jnp.concatenate([v_past.astype(v.dtype), v], axis=1)
        ki_all = jnp.concatenate([ki_past.astype(ki.dtype), ki], axis=1)
    conv_out, conv_state = conformer_conv(a, g, conv_hist, conv_w, conv_b, ln_g, ln_b)
    attn_out = sparse_attention(q, qi, wi, pos, k_all, v_all, ki_all)
    mixed = jnp.concatenate([conv_out, attn_out.reshape(B, T, ATT_WIDTH)], axis=-1) @ w_out
    return mixed, (k, v, ki, conv_state)


def encoder_layer(x, pos, past, lw):
    x = half_step_ffn(x, lw['ffn1_norm_pre'], lw['ffn1_norm_post'], lw['ffn1_w_up'], lw['ffn1_w_down'])
    h = rms_norm(x, lw['mix_norm_pre'])
    m, new_state = hybrid_mixer(h, pos, past, lw['w_in'], lw['conv_w'], lw['conv_b'],
                                lw['conv_ln_g'], lw['conv_ln_b'], lw['w_out'])
    x = x + rms_norm(m, lw['mix_norm_post'])
    x = half_step_ffn(x, lw['ffn2_norm_pre'], lw['ffn2_norm_post'], lw['ffn2_w_up'], lw['ffn2_w_down'])
    return x, new_state


def setup_inputs(seed: int = 0) -> dict:
    key = jax.random.key(seed)
    ks = jax.random.split(key, 24)
    f32 = jnp.float32

    def nrm(k, shape, scale):
        return jax.random.normal(k, shape, f32) * scale

    def gain(k, shape):
        return 1.0 + 0.02 * jax.random.normal(k, shape, f32)

    return {
        'x_prompt': nrm(ks[0], (BATCH, SEQ, D_MODEL), 1.0),
        'x_sample': nrm(ks[1], (DEC_BATCH, DEC_SEQ, D_MODEL), 1.0),
        'cache_k': nrm(ks[2], (DEPTH, DEC_BATCH, PAST_LEN, N_HEADS, HEAD_DIM), 1.0),
        'cache_v': nrm(ks[3], (DEPTH, DEC_BATCH, PAST_LEN, N_HEADS, HEAD_DIM), 1.0),
        'cache_kidx': nrm(ks[4], (DEPTH, DEC_BATCH, PAST_LEN, IDX_DIM), 1.0),
        'state_conv': nrm(ks[5], (DEPTH, DEC_BATCH, CONV_WIDTH - 1, C_CONV), 0.5),
        'ffn1_norm_pre': gain(ks[6], (DEPTH, D_MODEL)),
        'ffn1_norm_post': gain(ks[7], (DEPTH, D_MODEL)),
        'ffn1_w_up': nrm(ks[8], (DEPTH, D_MODEL, 2 * D_FF), D_MODEL ** -0.5),
        'ffn1_w_down': nrm(ks[9], (DEPTH, D_FF, D_MODEL), D_FF ** -0.5),
        'mix_norm_pre': gain(ks[10], (DEPTH, D_MODEL)),
        'mix_norm_post': gain(ks[11], (DEPTH, D_MODEL)),
        'w_in': nrm(ks[12], (DEPTH, D_MODEL, IN_COLS), D_MODEL ** -0.5),
        'conv_w': nrm(ks[13], (DEPTH, CONV_WIDTH, C_CONV), CONV_WIDTH ** -0.5),
        'conv_b': nrm(ks[14], (DEPTH, C_CONV), 0.02),
        'conv_ln_g': gain(ks[15], (DEPTH, C_CONV)),
        'conv_ln_b': nrm(ks[16], (DEPTH, C_CONV), 0.02),
        'w_out': nrm(ks[17], (DEPTH, MIX_WIDTH, D_MODEL), MIX_WIDTH ** -0.5),
        'ffn2_norm_pre': gain(ks[18], (DEPTH, D_MODEL)),
        'ffn2_norm_post': gain(ks[19], (DEPTH, D_MODEL)),
        'ffn2_w_up': nrm(ks[20], (DEPTH, D_MODEL, 2 * D_FF), D_MODEL ** -0.5),
        'ffn2_w_down': nrm(ks[21], (DEPTH, D_FF, D_MODEL), D_FF ** -0.5),
    }


def reference(x_prompt, x_sample, cache_k, cache_v, cache_kidx, state_conv,
              ffn1_norm_pre, ffn1_norm_post, ffn1_w_up, ffn1_w_down,
              mix_norm_pre, mix_norm_post, w_in, conv_w, conv_b, conv_ln_g, conv_ln_b, w_out,
              ffn2_norm_pre, ffn2_norm_post, ffn2_w_up, ffn2_w_down):
    past_len = cache_k.shape[2]
    pos_p = jnp.arange(x_prompt.shape[1])
    pos_s = past_len + jnp.arange(x_sample.shape[1])
    yp, ys = x_prompt, x_sample
    pk, pv, pki, pc = [], [], [], []
    sk, sv, ski, sc = [], [], [], []
    for l in range(DEPTH):
        lw = {
            'ffn1_norm_pre': ffn1_norm_pre[l], 'ffn1_norm_post': ffn1_norm_post[l],
            'ffn1_w_up': ffn1_w_up[l], 'ffn1_w_down': ffn1_w_down[l],
            'mix_norm_pre': mix_norm_pre[l], 'mix_norm_post': mix_norm_post[l],
            'w_in': w_in[l], 'conv_w': conv_w[l], 'conv_b': conv_b[l],
            'conv_ln_g': conv_ln_g[l], 'conv_ln_b': conv_ln_b[l], 'w_out': w_out[l],
            'ffn2_norm_pre': ffn2_norm_pre[l], 'ffn2_norm_post': ffn2_norm_post[l],
            'ffn2_w_up': ffn2_w_up[l], 'ffn2_w_down': ffn2_w_down[l],
        }
        yp, (k_p, v_p, ki_p, c_p) = encoder_layer(yp, pos_p, None, lw)
        ys, (k_s, v_s, ki_s, c_s) = encoder_layer(
            ys, pos_s, (cache_k[l], cache_v[l], cache_kidx[l], state_conv[l]), lw)
        pk.append(k_p); pv.append(v_p); pki.append(ki_p); pc.append(c_p)
        sk.append(k_s); sv.append(v_s); ski.append(ki_s); sc.append(c_s)
    return (yp, ys,
            jnp.stack(pk), jnp.stack(pv), jnp.stack(pki), jnp.stack(pc),
            jnp.stack(sk), jnp.stack(sv), jnp.stack(ski), jnp.stack(sc))
```

```python
import functools

import jax
import jax.numpy as jnp
from jax import lax
from jax.experimental import pallas as pl
from jax.experimental.pallas import tpu as pltpu

F32 = jnp.float32
BF16 = jnp.bfloat16
I32 = jnp.int32

CHUNK = 64
CHUNK_SHIFT = 6
CONV_WIDTH = 31
N_HEADS = 8
HEAD_DIM = 64
N_IDX_HEADS = 8
IDX_DIM = 64
MAX_TOPK = 256
ROPE_THETA = 10000.0
EPS = 1e-6
ATT_SCALE = HEAD_DIM ** -0.5
IDX_SCALE = IDX_DIM ** -0.5
IDX_HEAD_SCALE = N_IDX_HEADS ** -0.5

KEY_CHUNK = 256
Q_TILE = 256
HIST_PAD = 32
FF_CHUNK = 512
VMEM_LIMIT = 56 * 1024 * 1024

INT_MIN = -(2 ** 31)


def _rms(x, g):
    return x * lax.rsqrt(jnp.mean(x * x, axis=-1, keepdims=True) + EPS) * g


def _dot(a, b):
    return jnp.dot(a, b, preferred_element_type=F32)


def _dot_nt(a, b):
    return lax.dot_general(a, b, (((1,), (1,)), ((), ())), preferred_element_type=F32)


def _dot_tn(a, b):
    return lax.dot_general(a, b, (((0,), (0,)), ((), ())), preferred_element_type=F32)


def _ffn_body(x_ref, gpre_ref, gpost_ref, wup_ref, wdown_ref, o_ref, *, d_ff):
    x = x_ref[...]
    xn = _rms(x, gpre_ref[...]).astype(BF16)
    acc = None
    for c0 in range(0, d_ff, FF_CHUNK):
        c1 = min(c0 + FF_CHUNK, d_ff)
        gate = _dot(xn, wup_ref[:, c0:c1])
        up = _dot(xn, wup_ref[:, d_ff + c0:d_ff + c1])
        act = (gate * jax.nn.sigmoid(gate) * up).astype(BF16)
        part = _dot(act, wdown_ref[c0:c1, :])
        acc = part if acc is None else acc + part
    o_ref[...] = x + 0.5 * _rms(acc, gpost_ref[...])


def _ffn(x, g_pre, g_post, w_up, w_down):
    n, d = x.shape
    d_ff = w_down.shape[0]
    tm = min(512, n)
    assert n % tm == 0
    const = lambda i: (0, 0)
    return pl.pallas_call(
        functools.partial(_ffn_body, d_ff=d_ff),
        grid=(n // tm,),
        in_specs=[
            pl.BlockSpec((tm, d), lambda i: (i, 0)),
            pl.BlockSpec((1, d), const),
            pl.BlockSpec((1, d), const),
            pl.BlockSpec((d, 2 * d_ff), const, pipeline_mode=pl.Buffered(1)),
            pl.BlockSpec((d_ff, d), const, pipeline_mode=pl.Buffered(1)),
        ],
        out_specs=pl.BlockSpec((tm, d), lambda i: (i, 0)),
        out_shape=jax.ShapeDtypeStruct((n, d), F32),
        compiler_params=pltpu.CompilerParams(
            dimension_semantics=("parallel",), vmem_limit_bytes=VMEM_LIMIT),
        name="ffn",
    )(x, g_pre, g_post, w_up, w_down)


def _inproj_body(x_ref, g_ref, wrow_ref, wt_ref, cs_ref, sn_ref, ct_ref, st_ref, hist_ref,
                 cw_ref, cb_ref, lg_ref, lb_ref,
                 k_ref, v_ref, kidx_ref, cstate_ref, cout_ref, kbf_ref, kiext_ref,
                 qt_ref, qit_ref, vt_ref, wto_ref, ext_ref, *, tm, c_conv, att_w):
    i = pl.program_id(1)
    n_i = pl.num_programs(1)
    h = _rms(x_ref[...], g_ref[...]).astype(BF16)
    proj = _dot(h, wrow_ref[...])
    proj_t = _dot_nt(wt_ref[...], h)

    cs = cs_ref[...]
    sn = sn_ref[...]
    k0 = 2 * c_conv
    k_raw = proj[:, k0:k0 + att_w]
    lane = lax.broadcasted_iota(I32, (tm, att_w), 1)
    first_half = (lane & (HEAD_DIM - 1)) < (HEAD_DIM // 2)
    k_rot = jnp.where(first_half, pltpu.roll(k_raw, att_w - HEAD_DIM // 2, 1),
                      pltpu.roll(k_raw, HEAD_DIM // 2, 1))
    reps = att_w // 128
    k = k_raw * jnp.concatenate([cs] * reps, axis=1) + k_rot * jnp.concatenate([sn] * reps, axis=1)
    k_ref[...] = k
    kbf_ref[...] = k.astype(BF16)
    v_ref[...] = proj[:, k0 + att_w:k0 + 2 * att_w]

    ki_raw = proj[:, k0 + 2 * att_w:k0 + 2 * att_w + 128]
    lane1 = lax.broadcasted_iota(I32, (tm, 128), 1)
    first_half1 = (lane1 & (IDX_DIM - 1)) < (IDX_DIM // 2)
    ki_rot = jnp.where(first_half1, pltpu.roll(ki_raw, 128 - IDX_DIM // 2, 1),
                       pltpu.roll(ki_raw, IDX_DIM // 2, 1))
    ki = ki_raw * cs + ki_rot * sn
    kidx_ref[...] = ki[:, :IDX_DIM]
    ki_hi = ki.astype(BF16)
    ki_lo = ki - ki_hi.astype(F32)
    kiext_ref[:, 0:128] = (ki_hi.astype(F32) + pltpu.roll(ki_lo, IDX_DIM, 1)).astype(BF16)
    kiext_ref[:, 128:256] = ki_hi

    ct = ct_ref[...]
    st = st_ref[...]
    half = HEAD_DIM // 2
    zeros_half = jnp.zeros((HEAD_DIM, tm), BF16)
    for hd in range(N_HEADS):
        x1 = proj_t[hd * HEAD_DIM:hd * HEAD_DIM + half, :]
        x2 = proj_t[hd * HEAD_DIM + half:(hd + 1) * HEAD_DIM, :]
        base = hd * 128 + (hd % 2) * HEAD_DIM
        other = hd * 128 + ((hd + 1) % 2) * HEAD_DIM
        qt_ref[base:base + half, :] = ((x1 * ct - x2 * st) * ATT_SCALE).astype(BF16)
        qt_ref[base + half:base + HEAD_DIM, :] = ((x2 * ct + x1 * st) * ATT_SCALE).astype(BF16)
        qt_ref[other:other + HEAD_DIM, :] = zeros_half
    for hd in range(N_IDX_HEADS):
        r = att_w + hd * IDX_DIM
        x1 = proj_t[r:r + half, :]
        x2 = proj_t[r + half:r + IDX_DIM, :]
        y1 = (x1 * ct - x2 * st) * IDX_SCALE
        y2 = (x2 * ct + x1 * st) * IDX_SCALE
        y1h = y1.astype(BF16)
        y2h = y2.astype(BF16)
        y1l = (y1 - y1h.astype(F32)).astype(BF16)
        y2l = (y2 - y2h.astype(F32)).astype(BF16)
        b = hd * 256
        qit_ref[b:b + half, :] = y1h
        qit_ref[b + half:b + 64, :] = y2h
        qit_ref[b + 64:b + 64 + half, :] = y1h
        qit_ref[b + 64 + half:b + 128, :] = y2h
        qit_ref[b + 128:b + 128 + half, :] = y1l
        qit_ref[b + 128 + half:b + 192, :] = y2l
        qit_ref[b + 192:b + 256, :] = zeros_half
    vt_ref[...] = proj_t[2 * att_w:3 * att_w, :].astype(BF16)
    wto_ref[...] = proj_t[3 * att_w:3 * att_w + N_IDX_HEADS, :] * IDX_HEAD_SCALE

    a = proj[:, 0:c_conv]
    g = proj[:, c_conv:2 * c_conv]
    u = a * jax.nn.sigmoid(g)

    @pl.when(i == 0)
    def _():
        ext_ref[0:HIST_PAD, :] = hist_ref[...]

    @pl.when(i > 0)
    def _():
        ext_ref[0:HIST_PAD, :] = ext_ref[tm:tm + HIST_PAD, :]

    ext_ref[HIST_PAD:HIST_PAD + tm, :] = u
    cw = cw_ref[...]
    cb = cb_ref[...]
    lg = lg_ref[...]
    lb = lb_ref[...]
    lead = HIST_PAD - (CONV_WIDTH - 1)
    rs = min(64, tm)
    for r0 in range(0, tm, rs):
        acc = jnp.broadcast_to(cb, (rs, c_conv))
        for j in range(CONV_WIDTH):
            acc = acc + cw[j:j + 1, :] * ext_ref[r0 + j + lead:r0 + j + lead + rs, :]
        mu = jnp.mean(acc, axis=-1, keepdims=True)
        cen = acc - mu
        var = jnp.mean(cen * cen, axis=-1, keepdims=True)
        y = cen * lax.rsqrt(var + EPS) * lg + lb
        cout_ref[r0:r0 + rs, :] = (y * jax.nn.sigmoid(y)).astype(BF16)

    @pl.when(i == n_i - 1)
    def _():
        cstate_ref[...] = ext_ref[tm + lead:tm + HIST_PAD, :]


def _inproj(x, g_pre, w_row, w_t, cs, sn, ct, st, hist, conv_w, conv_b, ln_g, ln_b):
    b, t, d = x.shape
    c_conv = conv_w.shape[1]
    att_w = N_HEADS * HEAD_DIM
    tm = min(Q_TILE, t)
    assert t % tm == 0
    nt = t // tm
    const2 = lambda bi, i: (0, 0)
    row = lambda w: pl.BlockSpec((None, tm, w), lambda bi, i: (bi, i, 0))
    col = lambda r: pl.BlockSpec((None, r, tm), lambda bi, i: (bi, 0, i))
    out_shapes = (
        jax.ShapeDtypeStruct((b, t, att_w), F32),
        jax.ShapeDtypeStruct((b, t, att_w), F32),
        jax.ShapeDtypeStruct((b, t, IDX_DIM), F32),
        jax.ShapeDtypeStruct((b, CONV_WIDTH - 1, c_conv), F32),
        jax.ShapeDtypeStruct((b, t, c_conv), BF16),
        jax.ShapeDtypeStruct((b, t, att_w), BF16),
        jax.ShapeDtypeStruct((b, t, 256), BF16),
        jax.ShapeDtypeStruct((b, N_HEADS * 128, t), BF16),
        jax.ShapeDtypeStruct((b, N_IDX_HEADS * 256, t), BF16),
        jax.ShapeDtypeStruct((b, nt, att_w, tm), BF16),
        jax.ShapeDtypeStruct((b, N_IDX_HEADS, t), F32),
    )
    out_specs = (
        row(att_w), row(att_w), row(IDX_DIM),
        pl.BlockSpec((None, CONV_WIDTH - 1, c_conv), lambda bi, i: (bi, 0, 0)),
        row(c_conv), row(att_w), row(256),
        col(N_HEADS * 128), col(N_IDX_HEADS * 256),
        pl.BlockSpec((None, None, att_w, tm), lambda bi, i: (bi, i, 0, 0)),
        col(N_IDX_HEADS),
    )
    in_specs = [
        pl.BlockSpec((None, tm, d), lambda bi, i: (bi, i, 0)),
        pl.BlockSpec((1, d), const2),
        pl.BlockSpec(w_row.shape, const2, pipeline_mode=pl.Buffered(1)),
        pl.BlockSpec(w_t.shape, const2, pipeline_mode=pl.Buffered(1)),
        pl.BlockSpec((tm, 128), lambda bi, i: (i, 0)),
        pl.BlockSpec((tm, 128), lambda bi, i: (i, 0)),
        pl.BlockSpec((HEAD_DIM // 2, tm), lambda bi, i: (0, i)),
        pl.BlockSpec((HEAD_DIM // 2, tm), lambda bi, i: (0, i)),
        pl.BlockSpec((None, HIST_PAD, c_conv), lambda bi, i: (bi, 0, 0)),
        pl.BlockSpec(conv_w.shape, const2),
        pl.BlockSpec((1, c_conv), const2),
        pl.BlockSpec((1, c_conv), const2),
        pl.BlockSpec((1, c_conv), const2),
    ]
    return pl.pallas_call(
        functools.partial(_inproj_body, tm=tm, c_conv=c_conv, att_w=att_w),
        grid=(b, nt),
        in_specs=in_specs,
        out_specs=out_specs,
        out_shape=out_shapes,
        scratch_shapes=[pltpu.VMEM((tm + HIST_PAD, c_conv), F32)],
        compiler_params=pltpu.CompilerParams(
            dimension_semantics=("parallel", "arbitrary"), vmem_limit_bytes=VMEM_LIMIT),
        name="inproj_conv",
    )(x, g_pre, w_row, w_t, cs, sn, ct, st, hist, conv_w, conv_b, ln_g, ln_b)


def _key_to_f32(t):
    return lax.bitcast_convert_type(t ^ (lax.shift_right_arithmetic(t, 31) & 0x7FFFFFFF), F32)


def _seg_loop(segs, body, init):
    carry = init
    for base, n, rows in segs:
        if isinstance(n, int) and n == 1:
            carry = body(base, rows, carry)
        else:
            def step(kc, c, base=base, rows=rows):
                return body(pl.multiple_of(base + kc * rows, rows), rows, c)
            carry = lax.fori_loop(0, n, step, carry)
    return carry


def _count(score_ref, segs, tq, pred):
    def body(r0, rows, acc):
        s = score_ref[pl.ds(r0, rows), :]
        m = jnp.where(pred(s), 1, 0).astype(I32)
        return acc + jnp.sum(m.reshape(rows // 8, 8, tq), axis=0)
    acc = _seg_loop(segs, body, jnp.zeros((8, tq), I32))
    return jnp.sum(acc, axis=0, keepdims=True)


def _colmax(score_ref, segs, tq, pred):
    def body(r0, rows, acc):
        s = score_ref[pl.ds(r0, rows), :]
        m = jnp.where(pred(s), s, -jnp.inf)
        return jnp.maximum(acc, jnp.max(m.reshape(rows // 8, 8, tq), axis=0))
    acc = _seg_loop(segs, body, jnp.full((8, tq), -jnp.inf, F32))
    return jnp.max(acc, axis=0, keepdims=True)


def _select_bias(score_ref, bias_ref, segs, tq, topk):
    n_adm = _count(score_ref, segs, tq, lambda s: s > -jnp.inf)
    c0 = _count(score_ref, segs, tq, lambda s: s >= 0.0)
    p = jnp.where(c0 >= topk, 0, INT_MIN).astype(I32)

    def bit_body(b, p):
        t = p + lax.shift_left(jnp.int32(1), 30 - b)
        tf = _key_to_f32(t)
        c = _count(score_ref, segs, tq, lambda s: s >= tf)
        return jnp.where(c >= topk, t, p)

    p = lax.fori_loop(0, 31, bit_body, p)
    above = _key_to_f32(p + 1)
    v0 = _colmax(score_ref, segs, tq, lambda s: s < above)

    def walk_cond(st):
        return st[0] > 0

    def walk_body(st):
        _, v, thr, c_thr, fin = st
        c_ge = _count(score_ref, segs, tq, lambda s: s >= v)
        nxt = _colmax(score_ref, segs, tq, lambda s: s < v)
        ok = c_ge >= topk
        newly = ok & (fin == 0)
        thr = jnp.where(newly, v, thr)
        c_thr = jnp.where(newly, c_ge, c_thr)
        fin = jnp.where(ok, 1, fin)
        v = jnp.where(fin > 0, v, nxt)
        return jnp.sum(1 - fin), v, thr, c_thr, fin

    init = (jnp.int32(tq), v0, v0, jnp.zeros((1, tq), I32), jnp.zeros((1, tq), I32))
    _, _, thr, c_thr, _ = lax.while_loop(walk_cond, walk_body, init)

    n_gt = _count(score_ref, segs, tq, lambda s: s > thr)
    need = topk - n_gt
    take_all = n_adm <= topk
    gate = jnp.where(take_all, -jnp.inf, thr)
    tie = jnp.where(take_all, jnp.inf, thr)
    ambiguous = jnp.logical_not(take_all) & ((c_thr - n_gt) > need)
    n_amb = jnp.sum(jnp.where(ambiguous, 1, 0).astype(I32))

    @pl.when(n_amb == 0)
    def _():
        def body(r0, rows, c):
            s = score_ref[pl.ds(r0, rows), :]
            bias_ref[pl.ds(r0, rows), :] = jnp.where(
                s > gate, 0.0, jnp.where(s == tie, 0.0, -jnp.inf)).astype(F32)
            return c
        _seg_loop(segs, body, 0)

    @pl.when(n_amb > 0)
    def _():
        need_f = need.astype(F32)

        def body(r0, rows, seen):
            s = score_ref[pl.ds(r0, rows), :]
            eq = jnp.where(s == tie, 1.0, 0.0).astype(F32)
            ri = lax.broadcasted_iota(I32, (rows, rows), 0)
            ci = lax.broadcasted_iota(I32, (rows, rows), 1)
            tri = jnp.where(ci < ri, 1.0, 0.0).astype(BF16)
            before = _dot(tri, eq.astype(BF16)) + seen
            tie_bias = jnp.where(before < need_f, 0.0, -jnp.inf)
            bias_ref[pl.ds(r0, rows), :] = jnp.where(
                s > gate, 0.0, jnp.where(s == tie, tie_bias, -jnp.inf)).astype(F32)
            return seen + jnp.sum(eq, axis=0, keepdims=True)

        _seg_loop(segs, body, jnp.zeros((1, tq), F32))


def _masked_score(s, kpos, qpos):
    adm = lax.shift_right_arithmetic(kpos, CHUNK_SHIFT) <= lax.shift_right_arithmetic(qpos, CHUNK_SHIFT)
    return jnp.where(adm, s, -jnp.inf)


def _attn_prompt_body(kbf_ref, kiext_ref, vt_ref, qt_ref, qit_ref, wt_ref, out_ref,
                      score_ref, bias_ref, m_ref, l_ref, o_ref, att_ref, *, tq, topk):
    j = pl.program_id(1)
    segs = [(0, j + 1, KEY_CHUNK)]
    w = wt_ref[...]
    qpos = j * tq + lax.broadcasted_iota(I32, (1, tq), 1)

    def score_body(r0, rows, c):
        kie = kiext_ref[pl.ds(r0, rows), :]
        s = jnp.zeros((rows, tq), F32)
        for hd in range(N_IDX_HEADS):
            lgt = _dot(kie, qit_ref[hd * 256:(hd + 1) * 256, :])
            s = s + jnp.maximum(lgt, 0.0) * w[hd:hd + 1, :]
        kpos = r0 + lax.broadcasted_iota(I32, (rows, 1), 0)
        score_ref[pl.ds(r0, rows), :] = _masked_score(s, kpos, qpos)
        return c

    _seg_loop(segs, score_body, 0)
    _select_bias(score_ref, bias_ref, segs, tq, topk)

    m_ref[...] = jnp.full(m_ref.shape, -jnp.inf, F32)
    l_ref[...] = jnp.zeros(l_ref.shape, F32)
    o_ref[...] = jnp.zeros(o_ref.shape, F32)

    def qk_body(kc, c):
        r0 = pl.multiple_of(kc * KEY_CHUNK, KEY_CHUNK)
        bias = bias_ref[pl.ds(r0, KEY_CHUNK), :]
        for hd in range(N_HEADS):
            pair = hd // 2
            kb = kbf_ref[pl.ds(r0, KEY_CHUNK), pair * 128:(pair + 1) * 128]
            a = _dot(kb, qt_ref[hd * 128:(hd + 1) * 128, :]) + bias
            att_ref[hd, pl.ds(r0, KEY_CHUNK), :] = a
            ms = slice(hd * 8, (hd + 1) * 8)
            m_ref[ms, :] = jnp.maximum(m_ref[ms, :], jnp.max(a.reshape(KEY_CHUNK // 8, 8, tq), axis=0))
        return c

    lax.fori_loop(0, j + 1, qk_body, 0)
    for hd in range(N_HEADS):
        ms = slice(hd * 8, (hd + 1) * 8)
        m_ref[ms, :] = jnp.broadcast_to(jnp.max(m_ref[ms, :], axis=0, keepdims=True), (8, tq))

    def pv_body(kc, c):
        r0 = pl.multiple_of(kc * KEY_CHUNK, KEY_CHUNK)
        for hd in range(N_HEADS):
            ms = slice(hd * 8, (hd + 1) * 8)
            hs = slice(hd * HEAD_DIM, (hd + 1) * HEAD_DIM)
            p = jnp.exp(att_ref[hd, pl.ds(r0, KEY_CHUNK), :] - m_ref[hd * 8:hd * 8 + 1, :])
            l_ref[ms, :] = l_ref[ms, :] + jnp.sum(p.reshape(KEY_CHUNK // 8, 8, tq), axis=0)
            o_ref[hs, :] = o_ref[hs, :] + _dot(vt_ref[kc, hs, :], p.astype(BF16))
        return c

    lax.fori_loop(0, j + 1, pv_body, 0)
    for hd in range(N_HEADS):
        hs = slice(hd * HEAD_DIM, (hd + 1) * HEAD_DIM)
        l = jnp.sum(l_ref[hd * 8:(hd + 1) * 8, :], axis=0, keepdims=True)
        out_ref[hs, :] = (o_ref[hs, :] / l).astype(BF16)


def _attn_prompt(kbf, kiext, vt, qt, qit, wt):
    b, t, att_w = kbf.shape
    tq = Q_TILE
    assert t % tq == 0 and tq == KEY_CHUNK and vt.shape[3] == KEY_CHUNK
    nq = t // tq
    topk = min(MAX_TOPK, t // 4)
    full = lambda shape: pl.BlockSpec((None,) + shape, lambda bi, j: (bi,) + (0,) * len(shape))
    col = lambda r: pl.BlockSpec((None, r, tq), lambda bi, j: (bi, 0, j))
    return pl.pallas_call(
        functools.partial(_attn_prompt_body, tq=tq, topk=topk),
        grid=(b, nq),
        in_specs=[full((t, att_w)), full((t, 256)), full((nq, att_w, KEY_CHUNK)),
                  col(N_HEADS * 128), col(N_IDX_HEADS * 256), col(N_IDX_HEADS)],
        out_specs=col(att_w),
        out_shape=jax.ShapeDtypeStruct((b, att_w, t), BF16),
        scratch_shapes=[pltpu.VMEM((t, tq), F32), pltpu.VMEM((t, tq), F32),
                        pltpu.VMEM((N_HEADS * 8, tq), F32), pltpu.VMEM((N_HEADS * 8, tq), F32),
                        pltpu.VMEM((att_w, tq), F32), pltpu.VMEM((N_HEADS, t, tq), F32)],
        compiler_params=pltpu.CompilerParams(
            dimension_semantics=("parallel", "arbitrary"), vmem_limit_bytes=VMEM_LIMIT),
        name="attn_prompt",
    )(kbf, kiext, vt, qt, qit, wt)


def _attn_sample_body(ck_ref, cv_ref, cki_ref, kbf_ref, vnew_ref, kiext_ref, qt_ref, qit_ref, wt_ref,
                      out_ref, score_ref, bias_ref, att_ref, *, s_past, tq, topk):
    n_past = s_past // KEY_CHUNK
    segs = [(0, n_past, KEY_CHUNK), (s_past, 1, tq)]
    w = wt_ref[...]
    qpos = s_past + lax.broadcasted_iota(I32, (1, tq), 1)

    def score_past(r0, rows, c):
        ki = cki_ref[pl.ds(r0, rows), :]
        hi = ki.astype(BF16)
        lo = (ki - hi.astype(F32)).astype(BF16)
        s = jnp.zeros((rows, tq), F32)
        for hd in range(N_IDX_HEADS):
            q_hi = qit_ref[hd * 256:hd * 256 + IDX_DIM, :]
            q_lo = qit_ref[hd * 256 + 128:hd * 256 + 128 + IDX_DIM, :]
            lgt = _dot(hi, q_hi) + _dot(lo, q_hi) + _dot(hi, q_lo)
            s = s + jnp.maximum(lgt, 0.0) * w[hd:hd + 1, :]
        kpos = r0 + lax.broadcasted_iota(I32, (rows, 1), 0)
        score_ref[pl.ds(r0, rows), :] = _masked_score(s, kpos, qpos)
        return c

    _seg_loop(segs[:1], score_past, 0)
    kie = kiext_ref[...]
    s = jnp.zeros((tq, tq), F32)
    for hd in range(N_IDX_HEADS):
        lgt = _dot(kie, qit_ref[hd * 256:(hd + 1) * 256, :])
        s = s + jnp.maximum(lgt, 0.0) * w[hd:hd + 1, :]
    kpos = s_past + lax.broadcasted_iota(I32, (tq, 1), 0)
    score_ref[s_past:s_past + tq, :] = _masked_score(s, kpos, qpos)

    _select_bias(score_ref, bias_ref, segs, tq, topk)

    lane = lax.broadcasted_iota(I32, (tq, 128), 1)
    for hd in range(N_HEADS):
        wq = qt_ref[hd * 128:(hd + 1) * 128, :]
        pair = hd // 2

        def qk_body(r0, rows, m8, wq=wq, pair=pair):
            if rows == KEY_CHUNK:
                kb = ck_ref[pl.ds(r0, rows), pair * 128:(pair + 1) * 128].astype(BF16)
            else:
                kb = kbf_ref[:, pair * 128:(pair + 1) * 128]
            a = _dot(kb, wq) + bias_ref[pl.ds(r0, rows), :]
            att_ref[pl.ds(r0, rows), :] = a
            return jnp.maximum(m8, jnp.max(a.reshape(rows // 8, 8, tq), axis=0))

        m8 = _seg_loop(segs, qk_body, jnp.full((8, tq), -jnp.inf, F32))
        m = jnp.max(m8, axis=0, keepdims=True)

        def sum_body(r0, rows, l8, m=m):
            pexp = jnp.exp(att_ref[pl.ds(r0, rows), :] - m)
            return l8 + jnp.sum(pexp.reshape(rows // 8, 8, tq), axis=0)

        l = jnp.sum(_seg_loop(segs, sum_body, jnp.zeros((8, tq), F32)), axis=0, keepdims=True)
        inv_l = 1.0 / l

        def pv_body(r0, rows, o, m=m, inv_l=inv_l, pair=pair):
            p = (jnp.exp(att_ref[pl.ds(r0, rows), :] - m) * inv_l).astype(BF16)
            if rows == KEY_CHUNK:
                vb = cv_ref[pl.ds(r0, rows), pair * 128:(pair + 1) * 128].astype(BF16)
            else:
                vb = vnew_ref[:, pair * 128:(pair + 1) * 128].astype(BF16)
            return o + _dot_tn(p, vb)

        o = _seg_loop(segs, pv_body, jnp.zeros((tq, 128), F32))
        mine = (lane >= (hd % 2) * HEAD_DIM) & (lane < (hd % 2 + 1) * HEAD_DIM)
        if hd % 2 == 0:
            o_pair = jnp.where(mine, o, 0.0)
        else:
            out_ref[:, pair * 128:(pair + 1) * 128] = (o_pair + jnp.where(mine, o, 0.0)).astype(BF16)


def _attn_sample(cache_k, cache_v, cache_kidx, kbf, v_new, kiext, qt, qit, wt):
    b, s_past, att_w = cache_k.shape
    tq = kbf.shape[1]
    assert s_past % KEY_CHUNK == 0 and tq % 8 == 0
    s_all = s_past + tq
    topk = min(MAX_TOPK, s_all // 4)
    full = lambda shape: pl.BlockSpec((None,) + shape, lambda bi: (bi,) + (0,) * len(shape))
    return pl.pallas_call(
        functools.partial(_attn_sample_body, s_past=s_past, tq=tq, topk=topk),
        grid=(b,),
        in_specs=[full((s_past, att_w)), full((s_past, att_w)), full((s_past, IDX_DIM)),
                  full((tq, att_w)), full((tq, att_w)), full((tq, 256)),
                  full((N_HEADS * 128, tq)), full((N_IDX_HEADS * 256, tq)), full((N_IDX_HEADS, tq))],
        out_specs=full((tq, att_w)),
        out_shape=jax.ShapeDtypeStruct((b, tq, att_w), BF16),
        scratch_shapes=[pltpu.VMEM((s_all, tq), F32), pltpu.VMEM((s_all, tq), F32),
                        pltpu.VMEM((s_all, tq), F32)],
        compiler_params=pltpu.CompilerParams(
            dimension_semantics=("parallel",), vmem_limit_bytes=VMEM_LIMIT),
        name="attn_sample",
    )(cache_k, cache_v, cache_kidx, kbf, v_new, kiext, qt, qit, wt)


def _outproj_body(x_ref, cout_ref, att_ref, wc_ref, wa_ref, g_ref, o_ref, *, att_transposed):
    m = _dot(cout_ref[...], wc_ref[...])
    if att_transposed:
        m = m + _dot_tn(att_ref[...], wa_ref[...])
    else:
        m = m + _dot(att_ref[...], wa_ref[...])
    o_ref[...] = x_ref[...] + _rms(m, g_ref[...])


def _outproj(x, cout, att, w_conv, w_att, g_post, att_transposed):
    b, t, d = x.shape
    c_conv = cout.shape[2]
    att_w = w_att.shape[0]
    tm = min(512, t)
    assert t % tm == 0
    const2 = lambda bi, i: (0, 0)
    if att_transposed:
        att_spec = pl.BlockSpec((None, att_w, tm), lambda bi, i: (bi, 0, i))
    else:
        att_spec = pl.BlockSpec((None, tm, att_w), lambda bi, i: (bi, i, 0))
    return pl.pallas_call(
        functools.partial(_outproj_body, att_transposed=att_transposed),
        grid=(b, t // tm),
        in_specs=[
            pl.BlockSpec((None, tm, d), lambda bi, i: (bi, i, 0)),
            pl.BlockSpec((None, tm, c_conv), lambda bi, i: (bi, i, 0)),
            att_spec,
            pl.BlockSpec(w_conv.shape, const2),
            pl.BlockSpec(w_att.shape, const2),
            pl.BlockSpec((1, d), const2),
        ],
        out_specs=pl.BlockSpec((None, tm, d), lambda bi, i: (bi, i, 0)),
        out_shape=jax.ShapeDtypeStruct((b, t, d), F32),
        compiler_params=pltpu.CompilerParams(
            dimension_semantics=("parallel", "parallel"), vmem_limit_bytes=VMEM_LIMIT),
        name="outproj",
    )(x, cout, att, w_conv, w_att, g_post)


def _rope_tables(pos):
    inv = 1.0 / (ROPE_THETA ** (jnp.arange(0, HEAD_DIM, 2, dtype=F32) / HEAD_DIM))
    ang = pos.astype(F32)[:, None] * inv[None, :]
    cos, sin = jnp.cos(ang), jnp.sin(ang)
    cs = jnp.concatenate([cos, cos, cos, cos], axis=1)
    sn = jnp.concatenate([-sin, sin, -sin, sin], axis=1)
    return cs, sn, cos.T, sin.T


def _ffn_tokens(x, g_pre, g_post, w_up, w_down):
    b, t, d = x.shape
    return _ffn(x.reshape(b * t, d), g_pre, g_post, w_up, w_down).reshape(b, t, d)


def kernel(x_prompt, x_sample, cache_k, cache_v, cache_kidx, state_conv, ffn1_norm_pre, ffn1_norm_post, ffn1_w_up, ffn1_w_down, mix_norm_pre, mix_norm_post, w_in, conv_w, conv_b, conv_ln_g, conv_ln_b, w_out, ffn2_norm_pre, ffn2_norm_post, ffn2_w_up, ffn2_w_down):
    depth = w_in.shape[0]
    d = x_prompt.shape[2]
    c_conv = conv_w.shape[2]
    att_w = N_HEADS * HEAD_DIM
    assert IDX_DIM == HEAD_DIM and N_IDX_HEADS == N_HEADS
    assert w_in.shape[2] == 2 * c_conv + 3 * att_w + N_IDX_HEADS * IDX_DIM + IDX_DIM + N_IDX_HEADS
    bp, tp, _ = x_prompt.shape
    bs, ts, _ = x_sample.shape
    past_len = cache_k.shape[2]

    tabs_p = _rope_tables(jnp.arange(tp))
    tabs_s = _rope_tables(past_len + jnp.arange(ts))
    hist_p = jnp.zeros((bp, HIST_PAD, c_conv), F32)
    lead = HIST_PAD - (CONV_WIDTH - 1)

    yp, ys = x_prompt, x_sample
    outs = [[] for _ in range(8)]
    for l in range(depth):
        vec = lambda a: a[l][None, :]
        o_q = 2 * c_conv
        o_k = o_q + att_w
        o_v = o_k + att_w
        o_qi = o_v + att_w
        o_ki = o_qi + N_IDX_HEADS * IDX_DIM
        o_wi = o_ki + IDX_DIM
        wl = w_in[l]
        w_row = jnp.concatenate(
            [wl[:, :o_q], wl[:, o_k:o_qi], wl[:, o_ki:o_wi], jnp.zeros((d, 128 - IDX_DIM), F32)],
            axis=1).astype(BF16)
        w_t = jnp.concatenate(
            [wl[:, o_q:o_k], wl[:, o_qi:o_ki], wl[:, o_v:o_qi], wl[:, o_wi:],
             jnp.zeros((d, 16 - N_IDX_HEADS), F32)], axis=1).T.astype(BF16)
        w1u, w1d = ffn1_w_up[l].astype(BF16), ffn1_w_down[l].astype(BF16)
        w2u, w2d = ffn2_w_up[l].astype(BF16), ffn2_w_down[l].astype(BF16)
        wo_c, wo_a = w_out[l, :c_conv].astype(BF16), w_out[l, c_conv:].astype(BF16)
        hist_s = jnp.pad(state_conv[l], ((0, 0), (lead, 0), (0, 0)))

        yp = _ffn_tokens(yp, vec(ffn1_norm_pre), vec(ffn1_norm_post), w1u, w1d)
        ys = _ffn_tokens(ys, vec(ffn1_norm_pre), vec(ffn1_norm_post), w1u, w1d)

        conv_args = (conv_w[l], vec(conv_b), vec(conv_ln_g), vec(conv_ln_b))
        (k_p, v_p, ki_p, c_p, cout_p, kbf_p, kiext_p, qt_p, qit_p, vt_p, wt_p) = _inproj(
            yp, vec(mix_norm_pre), w_row, w_t, *tabs_p, hist_p, *conv_args)
        (k_s, v_s, ki_s, c_s, cout_s, kbf_s, kiext_s, qt_s, qit_s, _, wt_s) = _inproj(
            ys, vec(mix_norm_pre), w_row, w_t, *tabs_s, hist_s, *conv_args)

        att_p = _attn_prompt(kbf_p, kiext_p, vt_p, qt_p, qit_p, wt_p)
        att_s = _attn_sample(cache_k[l].reshape(bs, past_len, att_w), cache_v[l].reshape(bs, past_len, att_w),
                             cache_kidx[l], kbf_s, v_s, kiext_s, qt_s, qit_s, wt_s)

        yp = _outproj(yp, cout_p, att_p, wo_c, wo_a, vec(mix_norm_post), True)
        ys = _outproj(ys, cout_s, att_s, wo_c, wo_a, vec(mix_norm_post), False)

        yp = _ffn_tokens(yp, vec(ffn2_norm_pre), vec(ffn2_norm_post), w2u, w2d)
        ys = _ffn_tokens(ys, vec(ffn2_norm_pre), vec(ffn2_norm_post), w2u, w2d)

        for lst, val in zip(outs, (k_p.reshape(bp, tp, N_HEADS, HEAD_DIM), v_p.reshape(bp, tp, N_HEADS, HEAD_DIM),
                                   ki_p, c_p,
                                   k_s.reshape(bs, ts, N_HEADS, HEAD_DIM), v_s.reshape(bs, ts, N_HEADS, HEAD_DIM),
                                   ki_s, c_s)):
            lst.append(val)
    return (yp, ys) + tuple(jnp.stack(lst) for lst in outs)
```

```python
import functools

import jax
import jax.numpy as jnp
from jax import lax
from jax.experimental import pallas as pl
from jax.experimental.pallas import tpu as pltpu

F32 = jnp.float32
BF16 = jnp.bfloat16
I32 = jnp.int32

CHUNK_SHIFT = 6
CONV_WIDTH = 31
N_HEADS = 8
HEAD_DIM = 64
HEAD_SHIFT = 6
N_IDX_HEADS = 8
IDX_DIM = 64
MAX_TOPK = 256
ROPE_THETA = 10000.0
EPS = 1e-6
ATT_SCALE = HEAD_DIM ** -0.5
IDX_SCALE = IDX_DIM ** -0.5
IDX_HEAD_SCALE = N_IDX_HEADS ** -0.5

KEY_CHUNK = 256
Q_TILE = 256
NEW_KEY_PAD = 128
HIST_PAD = 32
CONV_ROWS = 32
FF_CHUNK = 512
VMEM_LIMIT = 56 * 1024 * 1024

INT_MIN = -(2 ** 31)


def _rms(x, g):
    return x * lax.rsqrt(jnp.mean(x * x, axis=-1, keepdims=True) + EPS) * g


def _dot(a, b):
    return jnp.dot(a, b, preferred_element_type=F32)


def _dot_nt(a, b):
    return lax.dot_general(a, b, (((1,), (1,)), ((), ())), preferred_element_type=F32)


def _dot_tn(a, b):
    return lax.dot_general(a, b, (((0,), (0,)), ((), ())), preferred_element_type=F32)


def _ffn_body(x_ref, gpre_ref, gpost_ref, wup_ref, wdown_ref, o_ref, *, d_ff):
    x = x_ref[...]
    xn = _rms(x, gpre_ref[...]).astype(BF16)
    acc = None
    for c0 in range(0, d_ff, FF_CHUNK):
        c1 = min(c0 + FF_CHUNK, d_ff)
        gate = _dot(xn, wup_ref[:, c0:c1])
        up = _dot(xn, wup_ref[:, d_ff + c0:d_ff + c1])
        act = (gate * jax.nn.sigmoid(gate) * up).astype(BF16)
        part = _dot(act, wdown_ref[c0:c1, :])
        acc = part if acc is None else acc + part
    o_ref[...] = x + 0.5 * _rms(acc, gpost_ref[...])


def _ffn(x, g_pre, g_post, w_up, w_down):
    n, d = x.shape
    d_ff = w_down.shape[0]
    tm = min(512, n)
    assert n % tm == 0
    const = lambda i: (0, 0)
    return pl.pallas_call(
        functools.partial(_ffn_body, d_ff=d_ff),
        grid=(n // tm,),
        in_specs=[
            pl.BlockSpec((tm, d), lambda i: (i, 0)),
            pl.BlockSpec((1, d), const),
            pl.BlockSpec((1, d), const),
            pl.BlockSpec((d, 2 * d_ff), const, pipeline_mode=pl.Buffered(1)),
            pl.BlockSpec((d_ff, d), const, pipeline_mode=pl.Buffered(1)),
        ],
        out_specs=pl.BlockSpec((tm, d), lambda i: (i, 0)),
        out_shape=jax.ShapeDtypeStruct((n, d), F32),
        compiler_params=pltpu.CompilerParams(
            dimension_semantics=("parallel",), vmem_limit_bytes=VMEM_LIMIT),
        name="ffn",
    )(x, g_pre, g_post, w_up, w_down)


def _rope_rows(x, cs, sn):
    tm, width = x.shape
    lane = lax.broadcasted_iota(I32, (tm, width), 1)
    first_half = (lane & (HEAD_DIM - 1)) < (HEAD_DIM // 2)
    rot = jnp.where(first_half, pltpu.roll(x, width - HEAD_DIM // 2, 1), pltpu.roll(x, HEAD_DIM // 2, 1))
    reps = width // 128
    return x * jnp.concatenate([cs] * reps, axis=1) + rot * jnp.concatenate([sn] * reps, axis=1)


def _inproj_body(*refs, tm, c_conv, att_w, transposed, n_alias):
    (x_ref, g_ref, wrow_ref, wq_ref, cs_ref, sn_ref, ct_ref, st_ref, hist_ref,
     cw_ref, cb_ref, lg_ref, lb_ref) = refs[:13]
    outs = refs[13 + n_alias:]
    k_ref, v_ref, kidx_ref, cstate_ref, cout_ref = outs[:5]
    ext_ref, shift_ref = outs[-2:]
    i = pl.program_id(1)
    n_i = pl.num_programs(1)
    h = _rms(x_ref[...], g_ref[...]).astype(BF16)
    proj = _dot(h, wrow_ref[...])

    cs = cs_ref[...]
    sn = sn_ref[...]
    k0 = 2 * c_conv
    k = _rope_rows(proj[:, k0:k0 + att_w], cs, sn)
    v = proj[:, k0 + att_w:k0 + 2 * att_w]
    for hd in range(N_HEADS):
        k_ref[pl.ds(hd, tm, stride=N_HEADS), :] = k[:, hd * HEAD_DIM:(hd + 1) * HEAD_DIM]
        v_ref[pl.ds(hd, tm, stride=N_HEADS), :] = v[:, hd * HEAD_DIM:(hd + 1) * HEAD_DIM]

    ki = _rope_rows(proj[:, k0 + 2 * att_w:k0 + 2 * att_w + 128], cs, sn)
    kidx_ref[...] = ki[:, :IDX_DIM]

    if transposed:
        kbf_ref, kiext_ref, qt_ref, qit_ref, vt_ref, wto_ref = outs[5:11]
        kbf_ref[...] = k.astype(BF16)
        ki_hi = ki.astype(BF16)
        ki_lo = ki - ki_hi.astype(F32)
        kiext_ref[:, 0:128] = (ki_hi.astype(F32) + pltpu.roll(ki_lo, IDX_DIM, 1)).astype(BF16)
        kiext_ref[:, 128:256] = ki_hi

        proj_t = _dot_nt(wq_ref[...], h)
        ct = ct_ref[...]
        st = st_ref[...]
        half = HEAD_DIM // 2
        zeros_half = jnp.zeros((HEAD_DIM, tm), BF16)
        for hd in range(N_HEADS):
            x1 = proj_t[hd * HEAD_DIM:hd * HEAD_DIM + half, :]
            x2 = proj_t[hd * HEAD_DIM + half:(hd + 1) * HEAD_DIM, :]
            base = hd * 128 + (hd % 2) * HEAD_DIM
            other = hd * 128 + ((hd + 1) % 2) * HEAD_DIM
            qt_ref[base:base + half, :] = ((x1 * ct - x2 * st) * ATT_SCALE).astype(BF16)
            qt_ref[base + half:base + HEAD_DIM, :] = ((x2 * ct + x1 * st) * ATT_SCALE).astype(BF16)
            qt_ref[other:other + HEAD_DIM, :] = zeros_half
        for hd in range(N_IDX_HEADS):
            r = att_w + hd * IDX_DIM
            x1 = proj_t[r:r + half, :]
            x2 = proj_t[r + half:r + IDX_DIM, :]
            y1 = (x1 * ct - x2 * st) * IDX_SCALE
            y2 = (x2 * ct + x1 * st) * IDX_SCALE
            y1h = y1.astype(BF16)
            y2h = y2.astype(BF16)
            y1l = (y1 - y1h.astype(F32)).astype(BF16)
            y2l = (y2 - y2h.astype(F32)).astype(BF16)
            b = hd * 256
            qit_ref[b:b + half, :] = y1h
            qit_ref[b + half:b + 64, :] = y2h
            qit_ref[b + 64:b + 64 + half, :] = y1h
            qit_ref[b + 64 + half:b + 128, :] = y2h
            qit_ref[b + 128:b + 128 + half, :] = y1l
            qit_ref[b + 128 + half:b + 192, :] = y2l
            qit_ref[b + 192:b + 256, :] = zeros_half
        vt_ref[...] = proj_t[2 * att_w:3 * att_w, :].astype(BF16)
        wto_ref[...] = proj_t[3 * att_w:3 * att_w + N_IDX_HEADS, :] * IDX_HEAD_SCALE
    else:
        knew_ref, vnew_ref, q_ref, qi_ref, w_ref = outs[5:10]
        knew_ref[...] = k
        vnew_ref[...] = v
        proj_q = _dot(h, wq_ref[...])
        q_ref[...] = _rope_rows(proj_q[:, 0:att_w], cs, sn) * ATT_SCALE
        qi_ref[...] = _rope_rows(proj_q[:, att_w:2 * att_w], cs, sn) * IDX_SCALE
        w_ref[...] = proj_q[:, 2 * att_w:2 * att_w + 128] * IDX_HEAD_SCALE

    a = proj[:, 0:c_conv]
    g = proj[:, c_conv:2 * c_conv]
    u = a * jax.nn.sigmoid(g)

    @pl.when(i == 0)
    def _():
        ext_ref[0:HIST_PAD, :] = hist_ref[...]

    @pl.when(i > 0)
    def _():
        ext_ref[0:HIST_PAD, :] = ext_ref[tm:tm + HIST_PAD, :]

    ext_ref[HIST_PAD:HIST_PAD + tm, :] = u
    cw = cw_ref[...]
    cb = cb_ref[...]
    lg = lg_ref[...]
    lb = lb_ref[...]
    lead = HIST_PAD - (CONV_WIDTH - 1)
    rs = min(CONV_ROWS, tm)
    for sh in range(1, 8):
        shift_ref[sh - 1] = ext_ref[sh:sh + tm + HIST_PAD - 8, :]
    for r0 in range(0, tm, rs):
        acc = jnp.broadcast_to(cb, (rs, c_conv))
        for j in range(CONV_WIDTH):
            sh = (j + lead) % 8
            a8 = r0 + (j + lead) // 8 * 8
            if sh == 0:
                tap = ext_ref[a8:a8 + rs, :]
            else:
                tap = shift_ref[sh - 1, a8:a8 + rs, :]
            acc = acc + cw[j:j + 1, :] * tap
        mu = jnp.mean(acc, axis=-1, keepdims=True)
        cen = acc - mu
        var = jnp.mean(cen * cen, axis=-1, keepdims=True)
        y = cen * lax.rsqrt(var + EPS) * lg + lb
        cout_ref[r0:r0 + rs, :] = (y * jax.nn.sigmoid(y)).astype(BF16)

    @pl.when(i == n_i - 1)
    def _():
        cstate_ref[...] = ext_ref[tm + lead:tm + HIST_PAD, :]


def _inproj(x, g_pre, w_row, w_q, tabs, hist, conv_w, conv_b, ln_g, ln_b, *, depth, layer, prev, transposed):
    b, t, d = x.shape
    c_conv = conv_w.shape[1]
    att_w = N_HEADS * HEAD_DIM
    tm = min(Q_TILE, t)
    assert t % tm == 0
    nt = t // tm
    cs, sn, ct, st = tabs
    const2 = lambda bi, i: (0, 0)
    row = lambda w: pl.BlockSpec((None, tm, w), lambda bi, i: (bi, i, 0))
    col = lambda r: pl.BlockSpec((None, r, tm), lambda bi, i: (bi, 0, i))
    stacked_shapes = (
        jax.ShapeDtypeStruct((depth, b, t * N_HEADS, HEAD_DIM), F32),
        jax.ShapeDtypeStruct((depth, b, t * N_HEADS, HEAD_DIM), F32),
        jax.ShapeDtypeStruct((depth, b, t, IDX_DIM), F32),
        jax.ShapeDtypeStruct((depth, b, CONV_WIDTH - 1, c_conv), F32),
    )
    stacked_specs = (
        pl.BlockSpec((None, None, tm * N_HEADS, HEAD_DIM), lambda bi, i: (layer, bi, i, 0)),
        pl.BlockSpec((None, None, tm * N_HEADS, HEAD_DIM), lambda bi, i: (layer, bi, i, 0)),
        pl.BlockSpec((None, None, tm, IDX_DIM), lambda bi, i: (layer, bi, i, 0)),
        pl.BlockSpec((None, None, CONV_WIDTH - 1, c_conv), lambda bi, i: (layer, bi, 0, 0)),
    )
    if transposed:
        extra_shapes = (
            jax.ShapeDtypeStruct((b, t, c_conv), BF16),
            jax.ShapeDtypeStruct((b, t, att_w), BF16),
            jax.ShapeDtypeStruct((b, t, 256), BF16),
            jax.ShapeDtypeStruct((b, N_HEADS * 128, t), BF16),
            jax.ShapeDtypeStruct((b, N_IDX_HEADS * 256, t), BF16),
            jax.ShapeDtypeStruct((b, nt, att_w, tm), BF16),
            jax.ShapeDtypeStruct((b, N_IDX_HEADS, t), F32),
        )
        extra_specs = (
            row(c_conv), row(att_w), row(256), col(N_HEADS * 128), col(N_IDX_HEADS * 256),
            pl.BlockSpec((None, None, att_w, tm), lambda bi, i: (bi, i, 0, 0)),
            col(N_IDX_HEADS),
        )
    else:
        extra_shapes = (
            jax.ShapeDtypeStruct((b, t, c_conv), BF16),
            jax.ShapeDtypeStruct((b, t, att_w), F32),
            jax.ShapeDtypeStruct((b, t, att_w), F32),
            jax.ShapeDtypeStruct((b, t, att_w), F32),
            jax.ShapeDtypeStruct((b, t, att_w), F32),
            jax.ShapeDtypeStruct((b, t, 128), F32),
        )
        extra_specs = (row(c_conv), row(att_w), row(att_w), row(att_w), row(att_w), row(128))
    in_specs = [
        pl.BlockSpec((None, tm, d), lambda bi, i: (bi, i, 0)),
        pl.BlockSpec((1, d), const2),
        pl.BlockSpec(w_row.shape, const2, pipeline_mode=pl.Buffered(1)),
        pl.BlockSpec(w_q.shape, const2, pipeline_mode=pl.Buffered(1)),
        pl.BlockSpec((tm, 128), lambda bi, i: (i, 0)),
        pl.BlockSpec((tm, 128), lambda bi, i: (i, 0)),
        pl.BlockSpec((HEAD_DIM // 2, tm), lambda bi, i: (0, i)),
        pl.BlockSpec((HEAD_DIM // 2, tm), lambda bi, i: (0, i)),
        pl.BlockSpec((None, HIST_PAD, c_conv), lambda bi, i: (bi, 0, 0)),
        pl.BlockSpec(conv_w.shape, const2),
        pl.BlockSpec((1, c_conv), const2),
        pl.BlockSpec((1, c_conv), const2),
        pl.BlockSpec((1, c_conv), const2),
    ]
    args = [x, g_pre, w_row, w_q, cs, sn, ct, st, hist, conv_w, conv_b, ln_g, ln_b]
    aliases = {}
    if prev is not None:
        for n, arr in enumerate(prev):
            aliases[len(args)] = n
            in_specs.append(pl.BlockSpec(memory_space=pl.ANY))
            args.append(arr)
    n_alias = 0 if prev is None else len(prev)
    res = pl.pallas_call(
        functools.partial(_inproj_body, tm=tm, c_conv=c_conv, att_w=att_w, transposed=transposed,
                          n_alias=n_alias),
        grid=(b, nt),
        in_specs=in_specs,
        out_specs=stacked_specs + extra_specs,
        out_shape=stacked_shapes + extra_shapes,
        scratch_shapes=[pltpu.VMEM((tm + HIST_PAD, c_conv), F32),
                        pltpu.VMEM((7, tm + HIST_PAD - 8, c_conv), F32)],
        input_output_aliases=aliases,
        compiler_params=pltpu.CompilerParams(
            dimension_semantics=("parallel", "arbitrary"), vmem_limit_bytes=VMEM_LIMIT),
        name="inproj_conv",
    )(*args)
    return res[:4], res[4:]


def _key_to_f32(t):
    return lax.bitcast_convert_type(t ^ (lax.shift_right_arithmetic(t, 31) & 0x7FFFFFFF), F32)


def _seg_loop(segs, body, init):
    carry = init
    for base, n, rows in segs:
        if isinstance(n, int) and n == 1:
            carry = body(base, rows, carry)
        else:
            def step(kc, c, base=base, rows=rows):
                return body(pl.multiple_of(base + kc * rows, rows), rows, c)
            carry = lax.fori_loop(0, n, step, carry)
    return carry


class _ColScores:
    def __init__(self, score_ref, bias_ref, segs, tq):
        self.score_ref, self.bias_ref, self.segs, self.tq = score_ref, bias_ref, segs, tq
        self.vec = (1, tq)

    def count(self, pred):
        tq = self.tq

        def body(r0, rows, acc):
            s = self.score_ref[pl.ds(r0, rows), :]
            m = jnp.where(pred(s), 1, 0).astype(I32)
            return acc + jnp.sum(m.reshape(rows // 8, 8, tq), axis=0)
        acc = _seg_loop(self.segs, body, jnp.zeros((8, tq), I32))
        return jnp.sum(acc, axis=0, keepdims=True)

    def count_and_max_below(self, bound):
        tq = self.tq

        def body(r0, rows, carry):
            cnt, mx = carry
            s = self.score_ref[pl.ds(r0, rows), :]
            ge = s >= bound
            cnt = cnt + jnp.sum(jnp.where(ge, 1, 0).astype(I32).reshape(rows // 8, 8, tq), axis=0)
            mx = jnp.maximum(mx, jnp.max(jnp.where(ge, -jnp.inf, s).reshape(rows // 8, 8, tq), axis=0))
            return cnt, mx
        cnt, mx = _seg_loop(self.segs, body,
                            (jnp.zeros((8, tq), I32), jnp.full((8, tq), -jnp.inf, F32)))
        return jnp.sum(cnt, axis=0, keepdims=True), jnp.max(mx, axis=0, keepdims=True)

    def write_bias(self, gate, tie, need_f):
        def body(r0, rows, seen):
            s = self.score_ref[pl.ds(r0, rows), :]
            if need_f is None:
                tie_bias = 0.0
            else:
                eq = jnp.where(s == tie, 1.0, 0.0).astype(F32)
                ri = lax.broadcasted_iota(I32, (rows, rows), 0)
                ci = lax.broadcasted_iota(I32, (rows, rows), 1)
                lower = jnp.where(ci < ri, 1.0, 0.0).astype(BF16)
                before = _dot(lower, eq.astype(BF16)) + seen
                tie_bias = jnp.where(before < need_f, 0.0, -jnp.inf)
                seen = seen + jnp.sum(eq, axis=0, keepdims=True)
            self.bias_ref[pl.ds(r0, rows), :] = jnp.where(
                s > gate, 0.0, jnp.where(s == tie, tie_bias, -jnp.inf)).astype(F32)
            return seen
        _seg_loop(self.segs, body, jnp.zeros((1, self.tq), F32))


class _RowScores:
    def __init__(self, score_ref, bias_ref, widths):
        self.score_ref, self.bias_ref, self.widths = score_ref, bias_ref, widths
        self.vec = (score_ref.shape[0], 1)

    def count(self, pred):
        s = self.score_ref[...]
        return jnp.sum(jnp.where(pred(s), 1.0, 0.0).astype(F32), axis=1, keepdims=True).astype(I32)

    def count_and_max_below(self, bound):
        s = self.score_ref[...]
        ge = s >= bound
        cnt = jnp.sum(jnp.where(ge, 1.0, 0.0).astype(F32), axis=1, keepdims=True).astype(I32)
        return cnt, jnp.max(jnp.where(ge, -jnp.inf, s), axis=1, keepdims=True)

    def write_bias(self, gate, tie, need_f):
        seen = jnp.zeros(self.vec, F32)
        c0 = 0
        for width in self.widths:
            s = self.score_ref[:, c0:c0 + width]
            if need_f is None:
                tie_bias = 0.0
            else:
                eq = jnp.where(s == tie, 1.0, 0.0).astype(F32)
                ri = lax.broadcasted_iota(I32, (width, width), 0)
                ci = lax.broadcasted_iota(I32, (width, width), 1)
                upper = jnp.where(ri < ci, 1.0, 0.0).astype(BF16)
                before = _dot(eq.astype(BF16), upper) + seen
                tie_bias = jnp.where(before < need_f, 0.0, -jnp.inf)
                seen = seen + jnp.sum(eq, axis=1, keepdims=True)
            self.bias_ref[:, c0:c0 + width] = jnp.where(
                s > gate, 0.0, jnp.where(s == tie, tie_bias, -jnp.inf)).astype(F32)
            c0 += width


def _select_bias(sc, topk, n_adm):
    c0 = sc.count(lambda s: s >= 0.0)
    nonneg = c0 >= topk
    p = jnp.where(nonneg, 0, INT_MIN).astype(I32)
    c_p = jnp.where(nonneg, c0, 0).astype(I32)

    def bit_body(b, carry):
        p, c_p = carry
        t = p + lax.shift_left(jnp.int32(1), 30 - b)
        tf = _key_to_f32(t)
        c = sc.count(lambda s: s >= tf)
        take = c >= topk
        return jnp.where(take, t, p), jnp.where(take, c, c_p)

    p, c_p = lax.fori_loop(0, 31, bit_body, (p, c_p))
    p_f = _key_to_f32(p)
    c_above, v0 = sc.count_and_max_below(_key_to_f32(p + 1))
    take_all = n_adm <= topk
    settled = take_all | (v0 == p_f)
    fin0 = jnp.where(settled, 1, 0).astype(I32)

    def walk_cond(st):
        return st[0] > 0

    def walk_body(st):
        _, v, cnt_above, thr, c_thr, n_gt, fin = st
        c_ge, nxt = sc.count_and_max_below(v)
        ok = c_ge >= topk
        newly = ok & (fin == 0)
        thr = jnp.where(newly, v, thr)
        c_thr = jnp.where(newly, c_ge, c_thr)
        n_gt = jnp.where(newly, cnt_above, n_gt)
        fin = jnp.where(ok, 1, fin)
        cnt_above = jnp.where(fin > 0, cnt_above, c_ge)
        v = jnp.where(fin > 0, v, nxt)
        return jnp.sum(1 - fin), v, cnt_above, thr, c_thr, n_gt, fin

    init = (jnp.sum(1 - fin0), v0, c_above, v0, c_p, c_above, fin0)
    _, _, _, thr, c_thr, n_gt, _ = lax.while_loop(walk_cond, walk_body, init)

    need = topk - n_gt
    gate = jnp.where(take_all, -jnp.inf, thr)
    tie = jnp.where(take_all, jnp.inf, thr)
    ambiguous = jnp.logical_not(take_all) & ((c_thr - n_gt) > need)
    n_amb = jnp.sum(jnp.where(ambiguous, 1, 0).astype(I32))

    @pl.when(n_amb == 0)
    def _():
        sc.write_bias(gate, tie, None)

    @pl.when(n_amb > 0)
    def _():
        sc.write_bias(gate, tie, need.astype(F32))


def _admissible_count(qpos, s_total):
    return jnp.minimum((lax.shift_right_arithmetic(qpos, CHUNK_SHIFT) + 1) * (1 << CHUNK_SHIFT), s_total)


def _attn_prompt_body(kbf_ref, kiext_ref, vt_ref, qt_ref, qit_ref, wt_ref, out_ref,
                      score_ref, bias_ref, m_ref, l_ref, o_ref, att_ref, *, tq, topk, s_total):
    j = pl.program_id(1)
    segs = [(0, j + 1, KEY_CHUNK)]
    w = wt_ref[...]
    qpos = j * tq + lax.broadcasted_iota(I32, (1, tq), 1)
    qchunk = lax.shift_right_arithmetic(qpos, CHUNK_SHIFT)

    def score_body(r0, rows, c):
        kie = kiext_ref[pl.ds(r0, rows), :]
        s = jnp.zeros((rows, tq), F32)
        for hd in range(N_IDX_HEADS):
            lgt = _dot(kie, qit_ref[hd * 256:(hd + 1) * 256, :])
            s = s + jnp.maximum(lgt, 0.0) * w[hd:hd + 1, :]
        kpos = r0 + lax.broadcasted_iota(I32, (rows, 1), 0)
        adm = lax.shift_right_arithmetic(kpos, CHUNK_SHIFT) <= qchunk
        score_ref[pl.ds(r0, rows), :] = jnp.where(adm, s, -jnp.inf)
        return c

    _seg_loop(segs, score_body, 0)
    _select_bias(_ColScores(score_ref, bias_ref, segs, tq), topk, _admissible_count(qpos, s_total))

    m_ref[...] = jnp.full(m_ref.shape, -jnp.inf, F32)
    l_ref[...] = jnp.zeros(l_ref.shape, F32)
    o_ref[...] = jnp.zeros(o_ref.shape, F32)

    def qk_body(kc, c):
        r0 = pl.multiple_of(kc * KEY_CHUNK, KEY_CHUNK)
        bias = bias_ref[pl.ds(r0, KEY_CHUNK), :]
        for hd in range(N_HEADS):
            pair = hd // 2
            kb = kbf_ref[pl.ds(r0, KEY_CHUNK), pair * 128:(pair + 1) * 128]
            a = _dot(kb, qt_ref[hd * 128:(hd + 1) * 128, :]) + bias
            att_ref[hd, pl.ds(r0, KEY_CHUNK), :] = a
            ms = slice(hd * 8, (hd + 1) * 8)
            m_ref[ms, :] = jnp.maximum(m_ref[ms, :], jnp.max(a.reshape(KEY_CHUNK // 8, 8, tq), axis=0))
        return c

    lax.fori_loop(0, j + 1, qk_body, 0)
    for hd in range(N_HEADS):
        ms = slice(hd * 8, (hd + 1) * 8)
        m_ref[ms, :] = jnp.broadcast_to(jnp.max(m_ref[ms, :], axis=0, keepdims=True), (8, tq))

    def pv_body(kc, c):
        r0 = pl.multiple_of(kc * KEY_CHUNK, KEY_CHUNK)
        for hd in range(N_HEADS):
            ms = slice(hd * 8, (hd + 1) * 8)
            hs = slice(hd * HEAD_DIM, (hd + 1) * HEAD_DIM)
            p = jnp.exp(att_ref[hd, pl.ds(r0, KEY_CHUNK), :] - m_ref[hd * 8:hd * 8 + 1, :])
            l_ref[ms, :] = l_ref[ms, :] + jnp.sum(p.reshape(KEY_CHUNK // 8, 8, tq), axis=0)
            o_ref[hs, :] = o_ref[hs, :] + _dot(vt_ref[kc, hs, :], p.astype(BF16))
        return c

    lax.fori_loop(0, j + 1, pv_body, 0)
    for hd in range(N_HEADS):
        hs = slice(hd * HEAD_DIM, (hd + 1) * HEAD_DIM)
        l = jnp.sum(l_ref[hd * 8:(hd + 1) * 8, :], axis=0, keepdims=True)
        out_ref[hs, :] = (o_ref[hs, :] / l).astype(BF16)


def _attn_prompt(kbf, kiext, vt, qt, qit, wt):
    b, t, att_w = kbf.shape
    tq = Q_TILE
    assert t % tq == 0 and tq == KEY_CHUNK and vt.shape[3] == KEY_CHUNK
    nq = t // tq
    topk = min(MAX_TOPK, t // 4)
    full = lambda shape: pl.BlockSpec((None,) + shape, lambda bi, j: (bi,) + (0,) * len(shape))
    col = lambda r: pl.BlockSpec((None, r, tq), lambda bi, j: (bi, 0, j))
    return pl.pallas_call(
        functools.partial(_attn_prompt_body, tq=tq, topk=topk, s_total=t),
        grid=(b, nq),
        in_specs=[full((t, att_w)), full((t, 256)), full((nq, att_w, KEY_CHUNK)),
                  col(N_HEADS * 128), col(N_IDX_HEADS * 256), col(N_IDX_HEADS)],
        out_specs=col(att_w),
        out_shape=jax.ShapeDtypeStruct((b, att_w, t), BF16),
        scratch_shapes=[pltpu.VMEM((t, tq), F32), pltpu.VMEM((t, tq), F32),
                        pltpu.VMEM((N_HEADS * 8, tq), F32), pltpu.VMEM((N_HEADS * 8, tq), F32),
                        pltpu.VMEM((att_w, tq), F32), pltpu.VMEM((N_HEADS, t, tq), F32)],
        compiler_params=pltpu.CompilerParams(
            dimension_semantics=("parallel", "arbitrary"), vmem_limit_bytes=VMEM_LIMIT),
        name="attn_prompt",
    )(kbf, kiext, vt, qt, qit, wt)


def _attn_sample_body(ck_ref, cv_ref, cki_ref, kn_ref, vn_ref, kin_ref, q_ref, qi_ref, w_ref, out_ref,
                      score_ref, bias_ref, att_ref, o_ref, *, s_past, ts, topk):
    s_total = s_past + ts
    rows = N_HEADS * ts
    blocks = [(c0, KEY_CHUNK) for c0 in range(0, s_past, KEY_CHUNK)] + [(s_past, NEW_KEY_PAD)]

    def key_block(ref_past, ref_new, c0, width):
        return ref_past[c0:c0 + width, :] if c0 < s_past else ref_new[...]

    def split(x):
        hi = x.astype(BF16)
        return hi, (x - hi.astype(F32)).astype(BF16)

    qi = qi_ref[...]
    qi_hi, qi_lo = split(jnp.concatenate(
        [qi[:, hd * IDX_DIM:(hd + 1) * IDX_DIM] for hd in range(N_IDX_HEADS)], axis=0))
    w = w_ref[...]
    qpos = s_past + lax.broadcasted_iota(I32, (ts, 1), 0)
    qchunk = lax.shift_right_arithmetic(qpos, CHUNK_SHIFT)
    for c0, width in blocks:
        ki_hi, ki_lo = split(key_block(cki_ref, kin_ref, c0, width))
        lgt = _dot_nt(qi_hi, ki_hi) + _dot_nt(qi_hi, ki_lo) + _dot_nt(qi_lo, ki_hi)
        s = jnp.zeros((ts, width), F32)
        for hd in range(N_IDX_HEADS):
            s = s + jnp.maximum(lgt[hd * ts:(hd + 1) * ts, :], 0.0) * w[:, hd:hd + 1]
        kpos = c0 + lax.broadcasted_iota(I32, (1, width), 1)
        adm = lax.shift_right_arithmetic(kpos, CHUNK_SHIFT) <= qchunk
        s = jnp.where(kpos < s_total, jnp.where(adm, s, -jnp.inf), -jnp.inf)
        score_ref[:, c0:c0 + width] = s

    _select_bias(_RowScores(score_ref, bias_ref, [wd for _, wd in blocks]), topk,
                 _admissible_count(qpos, s_total))

    q = q_ref[...]
    head_of_lane = lax.shift_right_arithmetic(lax.broadcasted_iota(I32, (ts, N_HEADS * HEAD_DIM), 1), HEAD_SHIFT)
    q_bd = jnp.concatenate([jnp.where(head_of_lane == hd, q, 0.0) for hd in range(N_HEADS)],
                           axis=0).astype(BF16)
    m_acc = jnp.full((rows, 128), -jnp.inf, F32)
    for c0, width in blocks:
        kb = key_block(ck_ref, kn_ref, c0, width).astype(BF16)
        a = _dot_nt(q_bd, kb)
        bias = bias_ref[:, c0:c0 + width]
        a = a + jnp.concatenate([bias] * N_HEADS, axis=0)
        att_ref[:, c0:c0 + width] = a
        for l0 in range(0, width, 128):
            m_acc = jnp.maximum(m_acc, a[:, l0:l0 + 128])
    m = jnp.max(m_acc, axis=1, keepdims=True)
    l_acc = jnp.zeros((rows, 128), F32)
    o_ref[...] = jnp.zeros(o_ref.shape, F32)
    for c0, width in blocks:
        p = jnp.exp(att_ref[:, c0:c0 + width] - m)
        for l0 in range(0, width, 128):
            l_acc = l_acc + p[:, l0:l0 + 128]
        vb = key_block(cv_ref, vn_ref, c0, width).astype(BF16)
        o_ref[...] = o_ref[...] + _dot(p.astype(BF16), vb)
    o = o_ref[...] / jnp.sum(l_acc, axis=1, keepdims=True)
    out = jnp.zeros((ts, N_HEADS * HEAD_DIM), F32)
    for hd in range(N_HEADS):
        out = out + jnp.where(head_of_lane == hd, o[hd * ts:(hd + 1) * ts, :], 0.0)
    out_ref[...] = out.astype(BF16)


def _attn_sample(cache_k, cache_v, cache_kidx, k_new, v_new, ki_new, q, qi, w):
    b, s_past, att_w = cache_k.shape
    ts = q.shape[1]
    assert s_past % KEY_CHUNK == 0 and ts % 8 == 0 and ts <= NEW_KEY_PAD
    assert k_new.shape[1] == NEW_KEY_PAD
    s_pad = s_past + NEW_KEY_PAD
    topk = min(MAX_TOPK, (s_past + ts) // 4)
    full = lambda shape: pl.BlockSpec((None,) + shape, lambda bi: (bi,) + (0,) * len(shape))
    return pl.pallas_call(
        functools.partial(_attn_sample_body, s_past=s_past, ts=ts, topk=topk),
        grid=(b,),
        in_specs=[full((s_past, att_w)), full((s_past, att_w)), full((s_past, IDX_DIM)),
                  full((NEW_KEY_PAD, att_w)), full((NEW_KEY_PAD, att_w)), full((NEW_KEY_PAD, IDX_DIM)),
                  full((ts, att_w)), full((ts, att_w)), full((ts, 128))],
        out_specs=full((ts, att_w)),
        out_shape=jax.ShapeDtypeStruct((b, ts, att_w), BF16),
        scratch_shapes=[pltpu.VMEM((ts, s_pad), F32), pltpu.VMEM((ts, s_pad), F32),
                        pltpu.VMEM((N_HEADS * ts, s_pad), F32), pltpu.VMEM((N_HEADS * ts, att_w), F32)],
        compiler_params=pltpu.CompilerParams(
            dimension_semantics=("parallel",), vmem_limit_bytes=VMEM_LIMIT),
        name="attn_sample",
    )(cache_k, cache_v, cache_kidx, k_new, v_new, ki_new, q, qi, w)


def _outproj_body(x_ref, cout_ref, att_ref, wc_ref, wa_ref, g_ref, o_ref, *, att_transposed):
    m = _dot(cout_ref[...], wc_ref[...])
    if att_transposed:
        m = m + _dot_tn(att_ref[...], wa_ref[...])
    else:
        m = m + _dot(att_ref[...], wa_ref[...])
    o_ref[...] = x_ref[...] + _rms(m, g_ref[...])


def _outproj(x, cout, att, w_conv, w_att, g_post, att_transposed):
    b, t, d = x.shape
    c_conv = cout.shape[2]
    att_w = w_att.shape[0]
    tm = min(512, t)
    assert t % tm == 0
    const2 = lambda bi, i: (0, 0)
    if att_transposed:
        att_spec = pl.BlockSpec((None, att_w, tm), lambda bi, i: (bi, 0, i))
    else:
        att_spec = pl.BlockSpec((None, tm, att_w), lambda bi, i: (bi, i, 0))
    return pl.pallas_call(
        functools.partial(_outproj_body, att_transposed=att_transposed),
        grid=(b, t // tm),
        in_specs=[
            pl.BlockSpec((None, tm, d), lambda bi, i: (bi, i, 0)),
            pl.BlockSpec((None, tm, c_conv), lambda bi, i: (bi, i, 0)),
            att_spec,
            pl.BlockSpec(w_conv.shape, const2),
            pl.BlockSpec(w_att.shape, const2),
            pl.BlockSpec((1, d), const2),
        ],
        out_specs=pl.BlockSpec((None, tm, d), lambda bi, i: (bi, i, 0)),
        out_shape=jax.ShapeDtypeStruct((b, t, d), F32),
        compiler_params=pltpu.CompilerParams(
            dimension_semantics=("parallel", "parallel"), vmem_limit_bytes=VMEM_LIMIT),
        name="outproj",
    )(x, cout, att, w_conv, w_att, g_post)


def _rope_tables(pos):
    inv = 1.0 / (ROPE_THETA ** (jnp.arange(0, HEAD_DIM, 2, dtype=F32) / HEAD_DIM))
    ang = pos.astype(F32)[:, None] * inv[None, :]
    cos, sin = jnp.cos(ang), jnp.sin(ang)
    cs = jnp.concatenate([cos, cos, cos, cos], axis=1)
    sn = jnp.concatenate([-sin, sin, -sin, sin], axis=1)
    return cs, sn, cos.T, sin.T


def _ffn_tokens(x, g_pre, g_post, w_up, w_down):
    b, t, d = x.shape
    return _ffn(x.reshape(b * t, d), g_pre, g_post, w_up, w_down).reshape(b, t, d)


def kernel(x_prompt, x_sample, cache_k, cache_v, cache_kidx, state_conv, ffn1_norm_pre, ffn1_norm_post, ffn1_w_up, ffn1_w_down, mix_norm_pre, mix_norm_post, w_in, conv_w, conv_b, conv_ln_g, conv_ln_b, w_out, ffn2_norm_pre, ffn2_norm_post, ffn2_w_up, ffn2_w_down):
    depth = w_in.shape[0]
    d = x_prompt.shape[2]
    c_conv = conv_w.shape[2]
    att_w = N_HEADS * HEAD_DIM
    assert IDX_DIM == HEAD_DIM and N_IDX_HEADS == N_HEADS
    assert w_in.shape[2] == 2 * c_conv + 3 * att_w + N_IDX_HEADS * IDX_DIM + IDX_DIM + N_IDX_HEADS
    bp, tp, _ = x_prompt.shape
    bs, ts, _ = x_sample.shape
    past_len = cache_k.shape[2]

    tabs_p = _rope_tables(jnp.arange(tp))
    tabs_s = _rope_tables(past_len + jnp.arange(ts))
    hist_p = jnp.zeros((bp, HIST_PAD, c_conv), F32)
    lead = HIST_PAD - (CONV_WIDTH - 1)
    pad_new = lambda a: jnp.pad(a, ((0, 0), (0, NEW_KEY_PAD - ts), (0, 0)))

    yp, ys = x_prompt, x_sample
    stack_p = stack_s = None
    for l in range(depth):
        vec = lambda a: a[l][None, :]
        o_q = 2 * c_conv
        o_k = o_q + att_w
        o_v = o_k + att_w
        o_qi = o_v + att_w
        o_ki = o_qi + N_IDX_HEADS * IDX_DIM
        o_wi = o_ki + IDX_DIM
        wl = w_in[l]
        w_row = jnp.concatenate(
            [wl[:, :o_q], wl[:, o_k:o_qi], wl[:, o_ki:o_wi], jnp.zeros((d, 128 - IDX_DIM), F32)],
            axis=1).astype(BF16)
        w_qt = jnp.concatenate(
            [wl[:, o_q:o_k], wl[:, o_qi:o_ki], wl[:, o_v:o_qi], wl[:, o_wi:],
             jnp.zeros((d, 16 - N_IDX_HEADS), F32)], axis=1).T.astype(BF16)
        w_qrow = jnp.concatenate(
            [wl[:, o_q:o_k], wl[:, o_qi:o_ki], wl[:, o_wi:], jnp.zeros((d, 128 - N_IDX_HEADS), F32)],
            axis=1).astype(BF16)
        w1u, w1d = ffn1_w_up[l].astype(BF16), ffn1_w_down[l].astype(BF16)
        w2u, w2d = ffn2_w_up[l].astype(BF16), ffn2_w_down[l].astype(BF16)
        wo_c, wo_a = w_out[l, :c_conv].astype(BF16), w_out[l, c_conv:].astype(BF16)
        hist_s = jnp.pad(state_conv[l], ((0, 0), (lead, 0), (0, 0)))

        yp = _ffn_tokens(yp, vec(ffn1_norm_pre), vec(ffn1_norm_post), w1u, w1d)
        ys = _ffn_tokens(ys, vec(ffn1_norm_pre), vec(ffn1_norm_post), w1u, w1d)

        conv_args = (conv_w[l], vec(conv_b), vec(conv_ln_g), vec(conv_ln_b))
        stack_p, (cout_p, kbf_p, kiext_p, qt_p, qit_p, vt_p, wt_p) = _inproj(
            yp, vec(mix_norm_pre), w_row, w_qt, tabs_p, hist_p, *conv_args,
            depth=depth, layer=l, prev=stack_p, transposed=True)
        stack_s, (cout_s, knew_s, vnew_s, q_s, qi_s, w_s) = _inproj(
            ys, vec(mix_norm_pre), w_row, w_qrow, tabs_s, hist_s, *conv_args,
            depth=depth, layer=l, prev=stack_s, transposed=False)

        att_p = _attn_prompt(kbf_p, kiext_p, vt_p, qt_p, qit_p, wt_p)
        att_s = _attn_sample(cache_k[l].reshape(bs, past_len, att_w), cache_v[l].reshape(bs, past_len, att_w),
                             cache_kidx[l], pad_new(knew_s), pad_new(vnew_s), pad_new(stack_s[2][l]),
                             q_s, qi_s, w_s)

        yp = _outproj(yp, cout_p, att_p, wo_c, wo_a, vec(mix_norm_post), True)
        ys = _outproj(ys, cout_s, att_s, wo_c, wo_a, vec(mix_norm_post), False)

        yp = _ffn_tokens(yp, vec(ffn2_norm_pre), vec(ffn2_norm_post), w2u, w2d)
        ys = _ffn_tokens(ys, vec(ffn2_norm_pre), vec(ffn2_norm_post), w2u, w2d)

    def heads(stack, b, t):
        k, v, kidx, cstate = stack
        shape = (depth, b, t, N_HEADS, HEAD_DIM)
        return k.reshape(shape), v.reshape(shape), kidx, cstate

    return (yp, ys) + heads(stack_p, bp, tp) + heads(stack_s, bs, ts)
```

```python
import functools

import jax
import jax.numpy as jnp
from jax import lax
from jax.experimental import pallas as pl
from jax.experimental.pallas import tpu as pltpu

F32 = jnp.float32
BF16 = jnp.bfloat16
I32 = jnp.int32

CHUNK_SHIFT = 6
CONV_WIDTH = 31
N_HEADS = 8
HEAD_DIM = 64
HEAD_SHIFT = 6
N_IDX_HEADS = 8
IDX_DIM = 64
MAX_TOPK = 256
ROPE_THETA = 10000.0
EPS = 1e-6
ATT_SCALE = HEAD_DIM ** -0.5
IDX_SCALE = IDX_DIM ** -0.5
IDX_HEAD_SCALE = N_IDX_HEADS ** -0.5

KEY_CHUNK = 256
Q_TILE = 256
NEW_KEY_PAD = 128
HIST_PAD = 32
CONV_ROWS = 32
FF_CHUNK = 512
VMEM_LIMIT = 56 * 1024 * 1024

ACC_ROWS = 32
INT_MIN = -(2 ** 31)


def _rms(x, g):
    return x * lax.rsqrt(jnp.mean(x * x, axis=-1, keepdims=True) + EPS) * g


def _dot(a, b):
    return jnp.dot(a, b, preferred_element_type=F32)


def _dot_nt(a, b):
    return lax.dot_general(a, b, (((1,), (1,)), ((), ())), preferred_element_type=F32)


def _dot_tn(a, b):
    return lax.dot_general(a, b, (((0,), (0,)), ((), ())), preferred_element_type=F32)


def _half_step_ffn(x, g_pre, g_post, wup_ref, wdown_ref, d_ff):
    xn = _rms(x, g_pre).astype(BF16)
    acc = None
    for c0 in range(0, d_ff, FF_CHUNK):
        c1 = min(c0 + FF_CHUNK, d_ff)
        gate = _dot(xn, wup_ref[:, c0:c1])
        up = _dot(xn, wup_ref[:, d_ff + c0:d_ff + c1])
        act = (gate * jax.nn.sigmoid(gate) * up).astype(BF16)
        part = _dot(act, wdown_ref[c0:c1, :])
        acc = part if acc is None else acc + part
    return x + 0.5 * _rms(acc, g_post)


def _ffn_body(x_ref, gpre_ref, gpost_ref, wup_ref, wdown_ref, o_ref, *, d_ff):
    o_ref[...] = _half_step_ffn(x_ref[...], gpre_ref[...], gpost_ref[...], wup_ref, wdown_ref, d_ff)


def _ffn(x, g_pre, g_post, w_up, w_down):
    n, d = x.shape
    d_ff = w_down.shape[0]
    tm = min(512, n)
    assert n % tm == 0
    const = lambda i: (0, 0)
    return pl.pallas_call(
        functools.partial(_ffn_body, d_ff=d_ff),
        grid=(n // tm,),
        in_specs=[
            pl.BlockSpec((tm, d), lambda i: (i, 0)),
            pl.BlockSpec((1, d), const),
            pl.BlockSpec((1, d), const),
            pl.BlockSpec((d, 2 * d_ff), const, pipeline_mode=pl.Buffered(1)),
            pl.BlockSpec((d_ff, d), const, pipeline_mode=pl.Buffered(1)),
        ],
        out_specs=pl.BlockSpec((tm, d), lambda i: (i, 0)),
        out_shape=jax.ShapeDtypeStruct((n, d), F32),
        compiler_params=pltpu.CompilerParams(
            dimension_semantics=("parallel",), vmem_limit_bytes=VMEM_LIMIT),
        name="ffn",
    )(x, g_pre, g_post, w_up, w_down)


def _rope_rows(x, cs, sn):
    tm, width = x.shape
    lane = lax.broadcasted_iota(I32, (tm, width), 1)
    first_half = (lane & (HEAD_DIM - 1)) < (HEAD_DIM // 2)
    rot = jnp.where(first_half, pltpu.roll(x, width - HEAD_DIM // 2, 1), pltpu.roll(x, HEAD_DIM // 2, 1))
    reps = width // 128
    return x * jnp.concatenate([cs] * reps, axis=1) + rot * jnp.concatenate([sn] * reps, axis=1)


def _inproj_body(*refs, tm, c_conv, att_w, transposed, n_alias):
    (x_ref, g_ref, wrow_ref, wq_ref, cs_ref, sn_ref, ct_ref, st_ref, hist_ref,
     cw_ref, cb_ref, lg_ref, lb_ref) = refs[:13]
    outs = refs[13 + n_alias:]
    k_ref, v_ref, kidx_ref, cstate_ref, cout_ref = outs[:5]
    ext_ref, shift_ref = outs[-2:]
    i = pl.program_id(1)
    n_i = pl.num_programs(1)
    h = _rms(x_ref[...], g_ref[...]).astype(BF16)
    proj = _dot(h, wrow_ref[...])

    cs = cs_ref[...]
    sn = sn_ref[...]
    k0 = 2 * c_conv
    k = _rope_rows(proj[:, k0:k0 + att_w], cs, sn)
    v = proj[:, k0 + att_w:k0 + 2 * att_w]
    for hd in range(N_HEADS):
        k_ref[pl.ds(hd, tm, stride=N_HEADS), :] = k[:, hd * HEAD_DIM:(hd + 1) * HEAD_DIM]
        v_ref[pl.ds(hd, tm, stride=N_HEADS), :] = v[:, hd * HEAD_DIM:(hd + 1) * HEAD_DIM]

    ki = _rope_rows(proj[:, k0 + 2 * att_w:k0 + 2 * att_w + 128], cs, sn)
    kidx_ref[...] = ki[:, :IDX_DIM]

    if transposed:
        kbf_ref, kiext_ref, qt_ref, qit_ref, vt_ref, wto_ref = outs[5:11]
        kbf_ref[...] = k.astype(BF16)
        ki_hi = ki.astype(BF16)
        ki_lo = ki - ki_hi.astype(F32)
        kiext_ref[:, 0:128] = (ki_hi.astype(F32) + pltpu.roll(ki_lo, IDX_DIM, 1)).astype(BF16)
        kiext_ref[:, 128:256] = ki_hi

        proj_t = _dot_nt(wq_ref[...], h)
        ct = ct_ref[...]
        st = st_ref[...]
        half = HEAD_DIM // 2
        zeros_half = jnp.zeros((HEAD_DIM, tm), BF16)
        for hd in range(N_HEADS):
            x1 = proj_t[hd * HEAD_DIM:hd * HEAD_DIM + half, :]
            x2 = proj_t[hd * HEAD_DIM + half:(hd + 1) * HEAD_DIM, :]
            base = hd * 128 + (hd % 2) * HEAD_DIM
            other = hd * 128 + ((hd + 1) % 2) * HEAD_DIM
            qt_ref[base:base + half, :] = ((x1 * ct - x2 * st) * ATT_SCALE).astype(BF16)
            qt_ref[base + half:base + HEAD_DIM, :] = ((x2 * ct + x1 * st) * ATT_SCALE).astype(BF16)
            qt_ref[other:other + HEAD_DIM, :] = zeros_half
        for hd in range(N_IDX_HEADS):
            r = att_w + hd * IDX_DIM
            x1 = proj_t[r:r + half, :]
            x2 = proj_t[r + half:r + IDX_DIM, :]
            y1 = (x1 * ct - x2 * st) * IDX_SCALE
            y2 = (x2 * ct + x1 * st) * IDX_SCALE
            y1h = y1.astype(BF16)
            y2h = y2.astype(BF16)
            y1l = (y1 - y1h.astype(F32)).astype(BF16)
            y2l = (y2 - y2h.astype(F32)).astype(BF16)
            b = hd * 256
            qit_ref[b:b + half, :] = y1h
            qit_ref[b + half:b + 64, :] = y2h
            qit_ref[b + 64:b + 64 + half, :] = y1h
            qit_ref[b + 64 + half:b + 128, :] = y2h
            qit_ref[b + 128:b + 128 + half, :] = y1l
            qit_ref[b + 128 + half:b + 192, :] = y2l
            qit_ref[b + 192:b + 256, :] = zeros_half
        vt_ref[...] = proj_t[2 * att_w:3 * att_w, :].astype(BF16)
        wto_ref[...] = proj_t[3 * att_w:3 * att_w + N_IDX_HEADS, :] * IDX_HEAD_SCALE
    else:
        knew_ref, vnew_ref, q_ref, qi_ref, w_ref = outs[5:10]
        knew_ref[...] = k
        vnew_ref[...] = v
        proj_q = _dot(h, wq_ref[...])
        q_ref[...] = _rope_rows(proj_q[:, 0:att_w], cs, sn) * ATT_SCALE
        qi_ref[...] = _rope_rows(proj_q[:, att_w:2 * att_w], cs, sn) * IDX_SCALE
        w_ref[...] = proj_q[:, 2 * att_w:2 * att_w + 128] * IDX_HEAD_SCALE

    a = proj[:, 0:c_conv]
    g = proj[:, c_conv:2 * c_conv]
    u = a * jax.nn.sigmoid(g)

    @pl.when(i == 0)
    def _():
        ext_ref[0:HIST_PAD, :] = hist_ref[...]

    @pl.when(i > 0)
    def _():
        ext_ref[0:HIST_PAD, :] = ext_ref[tm:tm + HIST_PAD, :]

    ext_ref[HIST_PAD:HIST_PAD + tm, :] = u
    cw = cw_ref[...]
    cb = cb_ref[...]
    lg = lg_ref[...]
    lb = lb_ref[...]
    lead = HIST_PAD - (CONV_WIDTH - 1)
    rs = min(CONV_ROWS, tm)
    for sh in range(1, 8):
        shift_ref[sh - 1] = ext_ref[sh:sh + tm + HIST_PAD - 8, :]
    for r0 in range(0, tm, rs):
        acc = jnp.broadcast_to(cb, (rs, c_conv))
        for j in range(CONV_WIDTH):
            sh = (j + lead) % 8
            a8 = r0 + (j + lead) // 8 * 8
            if sh == 0:
                tap = ext_ref[a8:a8 + rs, :]
            else:
                tap = shift_ref[sh - 1, a8:a8 + rs, :]
            acc = acc + cw[j:j + 1, :] * tap
        mu = jnp.mean(acc, axis=-1, keepdims=True)
        cen = acc - mu
        var = jnp.mean(cen * cen, axis=-1, keepdims=True)
        y = cen * lax.rsqrt(var + EPS) * lg + lb
        cout_ref[r0:r0 + rs, :] = (y * jax.nn.sigmoid(y)).astype(BF16)

    @pl.when(i == n_i - 1)
    def _():
        cstate_ref[...] = ext_ref[tm + lead:tm + HIST_PAD, :]


def _inproj(x, g_pre, w_row, w_q, tabs, hist, conv_w, conv_b, ln_g, ln_b, *, depth, layer, prev, transposed):
    b, t, d = x.shape
    c_conv = conv_w.shape[1]
    att_w = N_HEADS * HEAD_DIM
    tm = min(Q_TILE, t)
    assert t % tm == 0
    nt = t // tm
    cs, sn, ct, st = tabs
    const2 = lambda bi, i: (0, 0)
    row = lambda w: pl.BlockSpec((None, tm, w), lambda bi, i: (bi, i, 0))
    col = lambda r: pl.BlockSpec((None, r, tm), lambda bi, i: (bi, 0, i))
    stacked_shapes = (
        jax.ShapeDtypeStruct((depth, b, t * N_HEADS, HEAD_DIM), F32),
        jax.ShapeDtypeStruct((depth, b, t * N_HEADS, HEAD_DIM), F32),
        jax.ShapeDtypeStruct((depth, b, t, IDX_DIM), F32),
        jax.ShapeDtypeStruct((depth, b, CONV_WIDTH - 1, c_conv), F32),
    )
    stacked_specs = (
        pl.BlockSpec((None, None, tm * N_HEADS, HEAD_DIM), lambda bi, i: (layer, bi, i, 0)),
        pl.BlockSpec((None, None, tm * N_HEADS, HEAD_DIM), lambda bi, i: (layer, bi, i, 0)),
        pl.BlockSpec((None, None, tm, IDX_DIM), lambda bi, i: (layer, bi, i, 0)),
        pl.BlockSpec((None, None, CONV_WIDTH - 1, c_conv), lambda bi, i: (layer, bi, 0, 0)),
    )
    if transposed:
        extra_shapes = (
            jax.ShapeDtypeStruct((b, t, c_conv), BF16),
            jax.ShapeDtypeStruct((b, t, att_w), BF16),
            jax.ShapeDtypeStruct((b, t, 256), BF16),
            jax.ShapeDtypeStruct((b, N_HEADS * 128, t), BF16),
            jax.ShapeDtypeStruct((b, N_IDX_HEADS * 256, t), BF16),
            jax.ShapeDtypeStruct((b, nt, att_w, tm), BF16),
            jax.ShapeDtypeStruct((b, N_IDX_HEADS, t), F32),
        )
        extra_specs = (
            row(c_conv), row(att_w), row(256), col(N_HEADS * 128), col(N_IDX_HEADS * 256),
            pl.BlockSpec((None, None, att_w, tm), lambda bi, i: (bi, i, 0, 0)),
            col(N_IDX_HEADS),
        )
    else:
        extra_shapes = (
            jax.ShapeDtypeStruct((b, t, c_conv), BF16),
            jax.ShapeDtypeStruct((b, t, att_w), F32),
            jax.ShapeDtypeStruct((b, t, att_w), F32),
            jax.ShapeDtypeStruct((b, t, att_w), F32),
            jax.ShapeDtypeStruct((b, t, att_w), F32),
            jax.ShapeDtypeStruct((b, t, 128), F32),
        )
        extra_specs = (row(c_conv), row(att_w), row(att_w), row(att_w), row(att_w), row(128))
    in_specs = [
        pl.BlockSpec((None, tm, d), lambda bi, i: (bi, i, 0)),
        pl.BlockSpec((1, d), const2),
        pl.BlockSpec(w_row.shape, const2, pipeline_mode=pl.Buffered(1)),
        pl.BlockSpec(w_q.shape, const2, pipeline_mode=pl.Buffered(1)),
        pl.BlockSpec((tm, 128), lambda bi, i: (i, 0)),
        pl.BlockSpec((tm, 128), lambda bi, i: (i, 0)),
        pl.BlockSpec((HEAD_DIM // 2, tm), lambda bi, i: (0, i)),
        pl.BlockSpec((HEAD_DIM // 2, tm), lambda bi, i: (0, i)),
        pl.BlockSpec((None, HIST_PAD, c_conv), lambda bi, i: (bi, 0, 0)),
        pl.BlockSpec(conv_w.shape, const2),
        pl.BlockSpec((1, c_conv), const2),
        pl.BlockSpec((1, c_conv), const2),
        pl.BlockSpec((1, c_conv), const2),
    ]
    args = [x, g_pre, w_row, w_q, cs, sn, ct, st, hist, conv_w, conv_b, ln_g, ln_b]
    aliases = {}
    if prev is not None:
        for n, arr in enumerate(prev):
            aliases[len(args)] = n
            in_specs.append(pl.BlockSpec(memory_space=pl.ANY))
            args.append(arr)
    n_alias = 0 if prev is None else len(prev)
    res = pl.pallas_call(
        functools.partial(_inproj_body, tm=tm, c_conv=c_conv, att_w=att_w, transposed=transposed,
                          n_alias=n_alias),
        grid=(b, nt),
        in_specs=in_specs,
        out_specs=stacked_specs + extra_specs,
        out_shape=stacked_shapes + extra_shapes,
        scratch_shapes=[pltpu.VMEM((tm + HIST_PAD, c_conv), F32),
                        pltpu.VMEM((7, tm + HIST_PAD - 8, c_conv), F32)],
        input_output_aliases=aliases,
        compiler_params=pltpu.CompilerParams(
            dimension_semantics=("parallel", "arbitrary"), vmem_limit_bytes=VMEM_LIMIT),
        name="inproj_conv",
    )(*args)
    return res[:4], res[4:]


def _key_to_f32(t):
    return lax.bitcast_convert_type(t ^ (lax.shift_right_arithmetic(t, 31) & 0x7FFFFFFF), F32)


def _seg_loop(segs, body, init):
    carry = init
    for base, n, rows in segs:
        if isinstance(n, int) and n == 1:
            carry = body(base, rows, carry)
        else:
            def step(kc, c, base=base, rows=rows):
                return body(pl.multiple_of(base + kc * rows, rows), rows, c)
            carry = lax.fori_loop(0, n, step, carry)
    return carry


class _ColScores:
    def __init__(self, score_ref, bias_ref, segs, tq):
        self.score_ref, self.bias_ref, self.segs, self.tq = score_ref, bias_ref, segs, tq
        self.vec = (1, tq)

    def count(self, pred):
        tq = self.tq

        def body(r0, rows, acc):
            s = self.score_ref[pl.ds(r0, rows), :]
            m = jnp.where(pred(s), 1, 0).astype(I32)
            return acc + jnp.sum(m.reshape(rows // ACC_ROWS, ACC_ROWS, tq), axis=0)
        acc = _seg_loop(self.segs, body, jnp.zeros((ACC_ROWS, tq), I32))
        return jnp.sum(acc, axis=0, keepdims=True)

    def count_and_max_below(self, bound):
        tq = self.tq

        def body(r0, rows, carry):
            cnt, mx = carry
            s = self.score_ref[pl.ds(r0, rows), :]
            ge = s >= bound
            groups = (rows // ACC_ROWS, ACC_ROWS, tq)
            cnt = cnt + jnp.sum(jnp.where(ge, 1, 0).astype(I32).reshape(groups), axis=0)
            mx = jnp.maximum(mx, jnp.max(jnp.where(ge, -jnp.inf, s).reshape(groups), axis=0))
            return cnt, mx
        cnt, mx = _seg_loop(self.segs, body,
                            (jnp.zeros((ACC_ROWS, tq), I32), jnp.full((ACC_ROWS, tq), -jnp.inf, F32)))
        return jnp.sum(cnt, axis=0, keepdims=True), jnp.max(mx, axis=0, keepdims=True)

    def write_bias(self, gate, tie, need_f):
        def body(r0, rows, seen):
            s = self.score_ref[pl.ds(r0, rows), :]
            if need_f is None:
                tie_bias = 0.0
            else:
                eq = jnp.where(s == tie, 1.0, 0.0).astype(F32)
                ri = lax.broadcasted_iota(I32, (rows, rows), 0)
                ci = lax.broadcasted_iota(I32, (rows, rows), 1)
                lower = jnp.where(ci < ri, 1.0, 0.0).astype(BF16)
                before = _dot(lower, eq.astype(BF16)) + seen
                tie_bias = jnp.where(before < need_f, 0.0, -jnp.inf)
                seen = seen + jnp.sum(eq, axis=0, keepdims=True)
            self.bias_ref[pl.ds(r0, rows), :] = jnp.where(
                s > gate, 0.0, jnp.where(s == tie, tie_bias, -jnp.inf)).astype(F32)
            return seen
        _seg_loop(self.segs, body, jnp.zeros((1, self.tq), F32))


class _RowScores:
    def __init__(self, score_ref, bias_ref, widths):
        self.score_ref, self.bias_ref, self.widths = score_ref, bias_ref, widths
        self.vec = (score_ref.shape[0], 1)

    def count(self, pred):
        s = self.score_ref[...]
        return jnp.sum(jnp.where(pred(s), 1.0, 0.0).astype(F32), axis=1, keepdims=True).astype(I32)

    def count_and_max_below(self, bound):
        s = self.score_ref[...]
        ge = s >= bound
        cnt = jnp.sum(jnp.where(ge, 1.0, 0.0).astype(F32), axis=1, keepdims=True).astype(I32)
        return cnt, jnp.max(jnp.where(ge, -jnp.inf, s), axis=1, keepdims=True)

    def write_bias(self, gate, tie, need_f):
        seen = jnp.zeros(self.vec, F32)
        c0 = 0
        for width in self.widths:
            s = self.score_ref[:, c0:c0 + width]
            if need_f is None:
                tie_bias = 0.0
            else:
                eq = jnp.where(s == tie, 1.0, 0.0).astype(F32)
                ri = lax.broadcasted_iota(I32, (width, width), 0)
                ci = lax.broadcasted_iota(I32, (width, width), 1)
                upper = jnp.where(ri < ci, 1.0, 0.0).astype(BF16)
                before = _dot(eq.astype(BF16), upper) + seen
                tie_bias = jnp.where(before < need_f, 0.0, -jnp.inf)
                seen = seen + jnp.sum(eq, axis=1, keepdims=True)
            self.bias_ref[:, c0:c0 + width] = jnp.where(
                s > gate, 0.0, jnp.where(s == tie, tie_bias, -jnp.inf)).astype(F32)
            c0 += width


def _select_bias(sc, topk, n_adm):
    c0 = sc.count(lambda s: s >= 0.0)
    nonneg = c0 >= topk
    p = jnp.where(nonneg, 0, INT_MIN).astype(I32)
    c_p = jnp.where(nonneg, c0, 0).astype(I32)

    def bit_body(b, carry):
        p, c_p = carry
        t = p + lax.shift_left(jnp.int32(1), 30 - b)
        tf = _key_to_f32(t)
        c = sc.count(lambda s: s >= tf)
        take = c >= topk
        return jnp.where(take, t, p), jnp.where(take, c, c_p)

    p, c_p = lax.fori_loop(0, 31, bit_body, (p, c_p))
    p_f = _key_to_f32(p)
    c_above, v0 = sc.count_and_max_below(_key_to_f32(p + 1))
    take_all = n_adm <= topk
    settled = take_all | (v0 == p_f)
    fin0 = jnp.where(settled, 1, 0).astype(I32)

    def walk_cond(st):
        return st[0] > 0

    def walk_body(st):
        _, v, cnt_above, thr, c_thr, n_gt, fin = st
        c_ge, nxt = sc.count_and_max_below(v)
        ok = c_ge >= topk
        newly = ok & (fin == 0)
        thr = jnp.where(newly, v, thr)
        c_thr = jnp.where(newly, c_ge, c_thr)
        n_gt = jnp.where(newly, cnt_above, n_gt)
        fin = jnp.where(ok, 1, fin)
        cnt_above = jnp.where(fin > 0, cnt_above, c_ge)
        v = jnp.where(fin > 0, v, nxt)
        return jnp.sum(1 - fin), v, cnt_above, thr, c_thr, n_gt, fin

    init = (jnp.sum(1 - fin0), v0, c_above, v0, c_p, c_above, fin0)
    _, _, _, thr, c_thr, n_gt, _ = lax.while_loop(walk_cond, walk_body, init)

    need = topk - n_gt
    gate = jnp.where(take_all, -jnp.inf, thr)
    tie = jnp.where(take_all, jnp.inf, thr)
    ambiguous = jnp.logical_not(take_all) & ((c_thr - n_gt) > need)
    n_amb = jnp.sum(jnp.where(ambiguous, 1, 0).astype(I32))

    @pl.when(n_amb == 0)
    def _():
        sc.write_bias(gate, tie, None)

    @pl.when(n_amb > 0)
    def _():
        sc.write_bias(gate, tie, need.astype(F32))


def _admissible_count(qpos, s_total):
    return jnp.minimum((lax.shift_right_arithmetic(qpos, CHUNK_SHIFT) + 1) * (1 << CHUNK_SHIFT), s_total)


def _attn_prompt_body(kbf_ref, kiext_ref, vt_ref, qt_ref, qit_ref, wt_ref, out_ref,
                      score_ref, bias_ref, m_ref, l_ref, o_ref, att_ref, *, tq, topk, s_total):
    j = pl.program_id(1)
    segs = [(0, j + 1, KEY_CHUNK)]
    w = wt_ref[...]
    qpos = j * tq + lax.broadcasted_iota(I32, (1, tq), 1)
    qchunk = lax.shift_right_arithmetic(qpos, CHUNK_SHIFT)

    def score_body(r0, rows, c):
        kie = kiext_ref[pl.ds(r0, rows), :]
        s = jnp.zeros((rows, tq), F32)
        for hd in range(N_IDX_HEADS):
            lgt = _dot(kie, qit_ref[hd * 256:(hd + 1) * 256, :])
            s = s + jnp.maximum(lgt, 0.0) * w[hd:hd + 1, :]
        kpos = r0 + lax.broadcasted_iota(I32, (rows, 1), 0)
        adm = lax.shift_right_arithmetic(kpos, CHUNK_SHIFT) <= qchunk
        score_ref[pl.ds(r0, rows), :] = jnp.where(adm, s, -jnp.inf)
        return c

    _seg_loop(segs, score_body, 0)
    _select_bias(_ColScores(score_ref, bias_ref, segs, tq), topk, _admissible_count(qpos, s_total))

    m_ref[...] = jnp.full(m_ref.shape, -jnp.inf, F32)
    l_ref[...] = jnp.zeros(l_ref.shape, F32)
    o_ref[...] = jnp.zeros(o_ref.shape, F32)

    def qk_body(kc, c):
        r0 = pl.multiple_of(kc * KEY_CHUNK, KEY_CHUNK)
        bias = bias_ref[pl.ds(r0, KEY_CHUNK), :]
        for hd in range(N_HEADS):
            pair = hd // 2
            kb = kbf_ref[pl.ds(r0, KEY_CHUNK), pair * 128:(pair + 1) * 128]
            a = _dot(kb, qt_ref[hd * 128:(hd + 1) * 128, :]) + bias
            att_ref[hd, pl.ds(r0, KEY_CHUNK), :] = a
            ms = slice(hd * 8, (hd + 1) * 8)
            m_ref[ms, :] = jnp.maximum(m_ref[ms, :], jnp.max(a.reshape(KEY_CHUNK // 8, 8, tq), axis=0))
        return c

    lax.fori_loop(0, j + 1, qk_body, 0)
    for hd in range(N_HEADS):
        ms = slice(hd * 8, (hd + 1) * 8)
        m_ref[ms, :] = jnp.broadcast_to(jnp.max(m_ref[ms, :], axis=0, keepdims=True), (8, tq))

    def pv_body(kc, c):
        r0 = pl.multiple_of(kc * KEY_CHUNK, KEY_CHUNK)
        for hd in range(N_HEADS):
            ms = slice(hd * 8, (hd + 1) * 8)
            hs = slice(hd * HEAD_DIM, (hd + 1) * HEAD_DIM)
            p = jnp.exp(att_ref[hd, pl.ds(r0, KEY_CHUNK), :] - m_ref[hd * 8:hd * 8 + 1, :])
            l_ref[ms, :] = l_ref[ms, :] + jnp.sum(p.reshape(KEY_CHUNK // 8, 8, tq), axis=0)
            o_ref[hs, :] = o_ref[hs, :] + _dot(vt_ref[kc, hs, :], p.astype(BF16))
        return c

    lax.fori_loop(0, j + 1, pv_body, 0)
    for hd in range(N_HEADS):
        hs = slice(hd * HEAD_DIM, (hd + 1) * HEAD_DIM)
        l = jnp.sum(l_ref[hd * 8:(hd + 1) * 8, :], axis=0, keepdims=True)
        out_ref[hs, :] = (o_ref[hs, :] / l).astype(BF16)


def _attn_prompt(kbf, kiext, vt, qt, qit, wt):
    b, t, att_w = kbf.shape
    tq = Q_TILE
    assert t % tq == 0 and tq == KEY_CHUNK and vt.shape[3] == KEY_CHUNK
    nq = t // tq
    topk = min(MAX_TOPK, t // 4)
    full = lambda shape: pl.BlockSpec((None,) + shape, lambda bi, j: (bi,) + (0,) * len(shape))
    col = lambda r: pl.BlockSpec((None, r, tq), lambda bi, j: (bi, 0, j))
    return pl.pallas_call(
        functools.partial(_attn_prompt_body, tq=tq, topk=topk, s_total=t),
        grid=(b, nq),
        in_specs=[full((t, att_w)), full((t, 256)), full((nq, att_w, KEY_CHUNK)),
                  col(N_HEADS * 128), col(N_IDX_HEADS * 256), col(N_IDX_HEADS)],
        out_specs=col(att_w),
        out_shape=jax.ShapeDtypeStruct((b, att_w, t), BF16),
        scratch_shapes=[pltpu.VMEM((t, tq), F32), pltpu.VMEM((t, tq), F32),
                        pltpu.VMEM((N_HEADS * 8, tq), F32), pltpu.VMEM((N_HEADS * 8, tq), F32),
                        pltpu.VMEM((att_w, tq), F32), pltpu.VMEM((N_HEADS, t, tq), F32)],
        compiler_params=pltpu.CompilerParams(
            dimension_semantics=("parallel", "arbitrary"), vmem_limit_bytes=VMEM_LIMIT),
        name="attn_prompt",
    )(kbf, kiext, vt, qt, qit, wt)


def _attn_sample_body(ck_ref, cv_ref, cki_ref, kn_ref, vn_ref, kin_ref, q_ref, qi_ref, w_ref, out_ref,
                      score_ref, bias_ref, att_ref, o_ref, *, s_past, ts, topk):
    s_total = s_past + ts
    rows = N_HEADS * ts
    blocks = [(c0, KEY_CHUNK) for c0 in range(0, s_past, KEY_CHUNK)] + [(s_past, NEW_KEY_PAD)]

    def key_block(ref_past, ref_new, c0, width):
        return ref_past[c0:c0 + width, :] if c0 < s_past else ref_new[...]

    def split(x):
        hi = x.astype(BF16)
        return hi, (x - hi.astype(F32)).astype(BF16)

    qi = qi_ref[...]
    qi_hi, qi_lo = split(jnp.concatenate(
        [qi[:, hd * IDX_DIM:(hd + 1) * IDX_DIM] for hd in range(N_IDX_HEADS)], axis=0))
    w = w_ref[...]
    qpos = s_past + lax.broadcasted_iota(I32, (ts, 1), 0)
    qchunk = lax.shift_right_arithmetic(qpos, CHUNK_SHIFT)
    for c0, width in blocks:
        ki_hi, ki_lo = split(key_block(cki_ref, kin_ref, c0, width))
        lgt = _dot_nt(qi_hi, ki_hi) + _dot_nt(qi_hi, ki_lo) + _dot_nt(qi_lo, ki_hi)
        s = jnp.zeros((ts, width), F32)
        for hd in range(N_IDX_HEADS):
            s = s + jnp.maximum(lgt[hd * ts:(hd + 1) * ts, :], 0.0) * w[:, hd:hd + 1]
        kpos = c0 + lax.broadcasted_iota(I32, (1, width), 1)
        adm = lax.shift_right_arithmetic(kpos, CHUNK_SHIFT) <= qchunk
        s = jnp.where(kpos < s_total, jnp.where(adm, s, -jnp.inf), -jnp.inf)
        score_ref[:, c0:c0 + width] = s

    _select_bias(_RowScores(score_ref, bias_ref, [wd for _, wd in blocks]), topk,
                 _admissible_count(qpos, s_total))

    q = q_ref[...]
    head_of_lane = lax.shift_right_arithmetic(lax.broadcasted_iota(I32, (ts, N_HEADS * HEAD_DIM), 1), HEAD_SHIFT)
    q_bd = jnp.concatenate([jnp.where(head_of_lane == hd, q, 0.0) for hd in range(N_HEADS)],
                           axis=0).astype(BF16)
    m_acc = jnp.full((rows, 128), -jnp.inf, F32)
    for c0, width in blocks:
        kb = key_block(ck_ref, kn_ref, c0, width).astype(BF16)
        a = _dot_nt(q_bd, kb)
        bias = bias_ref[:, c0:c0 + width]
        a = a + jnp.concatenate([bias] * N_HEADS, axis=0)
        att_ref[:, c0:c0 + width] = a
        for l0 in range(0, width, 128):
            m_acc = jnp.maximum(m_acc, a[:, l0:l0 + 128])
    m = jnp.max(m_acc, axis=1, keepdims=True)
    l_acc = jnp.zeros((rows, 128), F32)
    o_ref[...] = jnp.zeros(o_ref.shape, F32)
    for c0, width in blocks:
        p = jnp.exp(att_ref[:, c0:c0 + width] - m)
        for l0 in range(0, width, 128):
            l_acc = l_acc + p[:, l0:l0 + 128]
        vb = key_block(cv_ref, vn_ref, c0, width).astype(BF16)
        o_ref[...] = o_ref[...] + _dot(p.astype(BF16), vb)
    o = o_ref[...] / jnp.sum(l_acc, axis=1, keepdims=True)
    out = jnp.zeros((ts, N_HEADS * HEAD_DIM), F32)
    for hd in range(N_HEADS):
        out = out + jnp.where(head_of_lane == hd, o[hd * ts:(hd + 1) * ts, :], 0.0)
    out_ref[...] = out.astype(BF16)


def _attn_sample(cache_k, cache_v, cache_kidx, k_new, v_new, ki_new, q, qi, w, *, layer):
    _, b, s_past, att_w = cache_k.shape
    ts = q.shape[1]
    assert s_past % KEY_CHUNK == 0 and ts % 8 == 0 and ts <= NEW_KEY_PAD
    assert k_new.shape[1] == NEW_KEY_PAD
    s_pad = s_past + NEW_KEY_PAD
    topk = min(MAX_TOPK, (s_past + ts) // 4)
    full = lambda shape: pl.BlockSpec((None,) + shape, lambda bi: (bi,) + (0,) * len(shape))
    cached = lambda width: pl.BlockSpec((None, None, s_past, width), lambda bi: (layer, bi, 0, 0))
    return pl.pallas_call(
        functools.partial(_attn_sample_body, s_past=s_past, ts=ts, topk=topk),
        grid=(b,),
        in_specs=[cached(att_w), cached(att_w), cached(IDX_DIM),
                  full((NEW_KEY_PAD, att_w)), full((NEW_KEY_PAD, att_w)), full((NEW_KEY_PAD, IDX_DIM)),
                  full((ts, att_w)), full((ts, att_w)), full((ts, 128))],
        out_specs=full((ts, att_w)),
        out_shape=jax.ShapeDtypeStruct((b, ts, att_w), BF16),
        scratch_shapes=[pltpu.VMEM((ts, s_pad), F32), pltpu.VMEM((ts, s_pad), F32),
                        pltpu.VMEM((N_HEADS * ts, s_pad), F32), pltpu.VMEM((N_HEADS * ts, att_w), F32)],
        compiler_params=pltpu.CompilerParams(
            dimension_semantics=("parallel",), vmem_limit_bytes=VMEM_LIMIT),
        name="attn_sample",
    )(cache_k, cache_v, cache_kidx, k_new, v_new, ki_new, q, qi, w)


def _outproj_ffn_body(x_ref, cout_ref, att_ref, wc_ref, wa_ref, g_ref, gpre_ref, gpost_ref, wup_ref, wdown_ref,
                      o_ref, *, att_transposed, d_ff):
    m = _dot(cout_ref[...], wc_ref[...])
    if att_transposed:
        m = m + _dot_tn(att_ref[...], wa_ref[...])
    else:
        m = m + _dot(att_ref[...], wa_ref[...])
    x = x_ref[...] + _rms(m, g_ref[...])
    o_ref[...] = _half_step_ffn(x, gpre_ref[...], gpost_ref[...], wup_ref, wdown_ref, d_ff)


def _outproj_ffn(x, cout, att, w_conv, w_att, g_post, att_transposed, g_ffn_pre, g_ffn_post, w_up, w_down):
    b, t, d = x.shape
    c_conv = cout.shape[2]
    att_w = w_att.shape[0]
    d_ff = w_down.shape[0]
    tm = min(512, t)
    assert t % tm == 0
    const2 = lambda bi, i: (0, 0)
    if att_transposed:
        att_spec = pl.BlockSpec((None, att_w, tm), lambda bi, i: (bi, 0, i))
    else:
        att_spec = pl.BlockSpec((None, tm, att_w), lambda bi, i: (bi, i, 0))
    return pl.pallas_call(
        functools.partial(_outproj_ffn_body, att_transposed=att_transposed, d_ff=d_ff),
        grid=(b, t // tm),
        in_specs=[
            pl.BlockSpec((None, tm, d), lambda bi, i: (bi, i, 0)),
            pl.BlockSpec((None, tm, c_conv), lambda bi, i: (bi, i, 0)),
            att_spec,
            pl.BlockSpec(w_conv.shape, const2, pipeline_mode=pl.Buffered(1)),
            pl.BlockSpec(w_att.shape, const2, pipeline_mode=pl.Buffered(1)),
            pl.BlockSpec((1, d), const2),
            pl.BlockSpec((1, d), const2),
            pl.BlockSpec((1, d), const2),
            pl.BlockSpec((d, 2 * d_ff), const2, pipeline_mode=pl.Buffered(1)),
            pl.BlockSpec((d_ff, d), const2, pipeline_mode=pl.Buffered(1)),
        ],
        out_specs=pl.BlockSpec((None, tm, d), lambda bi, i: (bi, i, 0)),
        out_shape=jax.ShapeDtypeStruct((b, t, d), F32),
        compiler_params=pltpu.CompilerParams(
            dimension_semantics=("parallel", "parallel"), vmem_limit_bytes=VMEM_LIMIT),
        name="outproj_ffn",
    )(x, cout, att, w_conv, w_att, g_post, g_ffn_pre, g_ffn_post, w_up, w_down)


def _rope_tables(pos):
    inv = 1.0 / (ROPE_THETA ** (jnp.arange(0, HEAD_DIM, 2, dtype=F32) / HEAD_DIM))
    ang = pos.astype(F32)[:, None] * inv[None, :]
    cos, sin = jnp.cos(ang), jnp.sin(ang)
    cs = jnp.concatenate([cos, cos, cos, cos], axis=1)
    sn = jnp.concatenate([-sin, sin, -sin, sin], axis=1)
    return cs, sn, cos.T, sin.T


def _ffn_tokens(x, g_pre, g_post, w_up, w_down):
    b, t, d = x.shape
    return _ffn(x.reshape(b * t, d), g_pre, g_post, w_up, w_down).reshape(b, t, d)


def kernel(x_prompt, x_sample, cache_k, cache_v, cache_kidx, state_conv, ffn1_norm_pre, ffn1_norm_post, ffn1_w_up, ffn1_w_down, mix_norm_pre, mix_norm_post, w_in, conv_w, conv_b, conv_ln_g, conv_ln_b, w_out, ffn2_norm_pre, ffn2_norm_post, ffn2_w_up, ffn2_w_down):
    depth = w_in.shape[0]
    d = x_prompt.shape[2]
    c_conv = conv_w.shape[2]
    att_w = N_HEADS * HEAD_DIM
    assert IDX_DIM == HEAD_DIM and N_IDX_HEADS == N_HEADS
    assert w_in.shape[2] == 2 * c_conv + 3 * att_w + N_IDX_HEADS * IDX_DIM + IDX_DIM + N_IDX_HEADS
    bp, tp, _ = x_prompt.shape
    bs, ts, _ = x_sample.shape
    past_len = cache_k.shape[2]

    tabs_p = _rope_tables(jnp.arange(tp))
    tabs_s = _rope_tables(past_len + jnp.arange(ts))
    hist_p = jnp.zeros((bp, HIST_PAD, c_conv), F32)
    lead = HIST_PAD - (CONV_WIDTH - 1)
    pad_new = lambda a: jnp.pad(a, ((0, 0), (0, NEW_KEY_PAD - ts), (0, 0)))
    ck_rows = cache_k.reshape(depth, bs, past_len, att_w)
    cv_rows = cache_v.reshape(depth, bs, past_len, att_w)

    yp, ys = x_prompt, x_sample
    stack_p = stack_s = None
    for l in range(depth):
        vec = lambda a: a[l][None, :]
        o_q = 2 * c_conv
        o_k = o_q + att_w
        o_v = o_k + att_w
        o_qi = o_v + att_w
        o_ki = o_qi + N_IDX_HEADS * IDX_DIM
        o_wi = o_ki + IDX_DIM
        wl = w_in[l]
        w_row = jnp.concatenate(
            [wl[:, :o_q], wl[:, o_k:o_qi], wl[:, o_ki:o_wi], jnp.zeros((d, 128 - IDX_DIM), F32)],
            axis=1).astype(BF16)
        w_qt = jnp.concatenate(
            [wl[:, o_q:o_k], wl[:, o_qi:o_ki], wl[:, o_v:o_qi], wl[:, o_wi:],
             jnp.zeros((d, 16 - N_IDX_HEADS), F32)], axis=1).T.astype(BF16)
        w_qrow = jnp.concatenate(
            [wl[:, o_q:o_k], wl[:, o_qi:o_ki], wl[:, o_wi:], jnp.zeros((d, 128 - N_IDX_HEADS), F32)],
            axis=1).astype(BF16)
        w1u, w1d = ffn1_w_up[l].astype(BF16), ffn1_w_down[l].astype(BF16)
        w2u, w2d = ffn2_w_up[l].astype(BF16), ffn2_w_down[l].astype(BF16)
        wo_c, wo_a = w_out[l, :c_conv].astype(BF16), w_out[l, c_conv:].astype(BF16)
        hist_s = jnp.pad(state_conv[l], ((0, 0), (lead, 0), (0, 0)))

        yp = _ffn_tokens(yp, vec(ffn1_norm_pre), vec(ffn1_norm_post), w1u, w1d)
        ys = _ffn_tokens(ys, vec(ffn1_norm_pre), vec(ffn1_norm_post), w1u, w1d)

        conv_args = (conv_w[l], vec(conv_b), vec(conv_ln_g), vec(conv_ln_b))
        stack_p, (cout_p, kbf_p, kiext_p, qt_p, qit_p, vt_p, wt_p) = _inproj(
            yp, vec(mix_norm_pre), w_row, w_qt, tabs_p, hist_p, *conv_args,
            depth=depth, layer=l, prev=stack_p, transposed=True)
        stack_s, (cout_s, knew_s, vnew_s, q_s, qi_s, w_s) = _inproj(
            ys, vec(mix_norm_pre), w_row, w_qrow, tabs_s, hist_s, *conv_args,
            depth=depth, layer=l, prev=stack_s, transposed=False)

        att_p = _attn_prompt(kbf_p, kiext_p, vt_p, qt_p, qit_p, wt_p)
        att_s = _attn_sample(ck_rows, cv_rows, cache_kidx, pad_new(knew_s), pad_new(vnew_s),
                             pad_new(stack_s[2][l]), q_s, qi_s, w_s, layer=l)

        ffn2 = (vec(ffn2_norm_pre), vec(ffn2_norm_post), w2u, w2d)
        yp = _outproj_ffn(yp, cout_p, att_p, wo_c, wo_a, vec(mix_norm_post), True, *ffn2)
        ys = _outproj_ffn(ys, cout_s, att_s, wo_c, wo_a, vec(mix_norm_post), False, *ffn2)

    def heads(stack, b, t):
        k, v, kidx, cstate = stack
        shape = (depth, b, t, N_HEADS, HEAD_DIM)
        return k.reshape(shape), v.reshape(shape), kidx, cstate

    return (yp, ys) + heads(stack_p, bp, tp) + heads(stack_s, bs, ts)
```

```python
import functools

import jax
import jax.numpy as jnp
from jax import lax
from jax.experimental import pallas as pl
from jax.experimental.pallas import tpu as pltpu

F32 = jnp.float32
BF16 = jnp.bfloat16
I32 = jnp.int32

CHUNK_SHIFT = 6
CONV_WIDTH = 31
N_HEADS = 8
HEAD_DIM = 64
HEAD_SHIFT = 6
N_IDX_HEADS = 8
IDX_DIM = 64
MAX_TOPK = 256
ROPE_THETA = 10000.0
EPS = 1e-6
ATT_SCALE = HEAD_DIM ** -0.5
IDX_SCALE = IDX_DIM ** -0.5
IDX_HEAD_SCALE = N_IDX_HEADS ** -0.5

KEY_CHUNK = 256
Q_TILE = 256
NEW_KEY_PAD = 128
HIST_PAD = 32
CONV_ROWS = 32
FF_CHUNK = 512
VMEM_LIMIT = 56 * 1024 * 1024

ACC_ROWS = 32
INT_MIN = -(2 ** 31)


def _rms(x, g):
    return x * lax.rsqrt(jnp.mean(x * x, axis=-1, keepdims=True) + EPS) * g


def _dot(a, b):
    return jnp.dot(a, b, preferred_element_type=F32)


def _dot_nt(a, b):
    return lax.dot_general(a, b, (((1,), (1,)), ((), ())), preferred_element_type=F32)


def _dot_tn(a, b):
    return lax.dot_general(a, b, (((0,), (0,)), ((), ())), preferred_element_type=F32)


def _half_step_ffn(x, g_pre, g_post, wup_ref, wdown_ref, d_ff):
    xn = _rms(x, g_pre).astype(BF16)
    acc = None
    for c0 in range(0, d_ff, FF_CHUNK):
        c1 = min(c0 + FF_CHUNK, d_ff)
        gate = _dot(xn, wup_ref[:, c0:c1])
        up = _dot(xn, wup_ref[:, d_ff + c0:d_ff + c1])
        act = (gate * jax.nn.sigmoid(gate) * up).astype(BF16)
        part = _dot(act, wdown_ref[c0:c1, :])
        acc = part if acc is None else acc + part
    return x + 0.5 * _rms(acc, g_post)


def _ffn_body(x_ref, gpre_ref, gpost_ref, wup_ref, wdown_ref, o_ref, *, d_ff):
    o_ref[...] = _half_step_ffn(x_ref[...], gpre_ref[...], gpost_ref[...], wup_ref, wdown_ref, d_ff)


def _ffn(x, g_pre, g_post, w_up, w_down):
    n, d = x.shape
    d_ff = w_down.shape[0]
    tm = min(512, n)
    assert n % tm == 0
    const = lambda i: (0, 0)
    return pl.pallas_call(
        functools.partial(_ffn_body, d_ff=d_ff),
        grid=(n // tm,),
        in_specs=[
            pl.BlockSpec((tm, d), lambda i: (i, 0)),
            pl.BlockSpec((1, d), const),
            pl.BlockSpec((1, d), const),
            pl.BlockSpec((d, 2 * d_ff), const, pipeline_mode=pl.Buffered(1)),
            pl.BlockSpec((d_ff, d), const, pipeline_mode=pl.Buffered(1)),
        ],
        out_specs=pl.BlockSpec((tm, d), lambda i: (i, 0)),
        out_shape=jax.ShapeDtypeStruct((n, d), F32),
        compiler_params=pltpu.CompilerParams(
            dimension_semantics=("parallel",), vmem_limit_bytes=VMEM_LIMIT),
        name="ffn",
    )(x, g_pre, g_post, w_up, w_down)


def _rope_rows(x, cs, sn):
    tm, width = x.shape
    lane = lax.broadcasted_iota(I32, (tm, width), 1)
    first_half = (lane & (HEAD_DIM - 1)) < (HEAD_DIM // 2)
    rot = jnp.where(first_half, pltpu.roll(x, width - HEAD_DIM // 2, 1), pltpu.roll(x, HEAD_DIM // 2, 1))
    reps = width // 128
    return x * jnp.concatenate([cs] * reps, axis=1) + rot * jnp.concatenate([sn] * reps, axis=1)


def _inproj_body(*refs, tm, c_conv, att_w, transposed, n_alias):
    (x_ref, g_ref, wrow_ref, wq_ref, cs_ref, sn_ref, ct_ref, st_ref, hist_ref,
     cw_ref, cb_ref, lg_ref, lb_ref) = refs[:13]
    outs = refs[13 + n_alias:]
    k_ref, v_ref, kidx_ref, cstate_ref, cout_ref = outs[:5]
    ext_ref, shift_ref = outs[-2:]
    i = pl.program_id(1)
    n_i = pl.num_programs(1)
    @pl.when(i == 0)
    def _():
        ext_ref[0:HIST_PAD, :] = hist_ref[...]

    @pl.when(i > 0)
    def _():
        ext_ref[0:HIST_PAD, :] = ext_ref[tm:tm + HIST_PAD, :]

    h = _rms(x_ref[...], g_ref[...]).astype(BF16)
    k0 = 2 * c_conv
    proj_ag = _dot(h, wrow_ref[:, 0:k0])
    proj_kv = _dot(h, wrow_ref[:, k0:k0 + 2 * att_w])
    proj_ki = _dot(h, wrow_ref[:, k0 + 2 * att_w:k0 + 2 * att_w + 128])
    if transposed:
        q_t = _dot_nt(wq_ref[0:att_w, :], h)
        qi_t = _dot_nt(wq_ref[att_w:2 * att_w, :], h)
        v_t = _dot_nt(wq_ref[2 * att_w:3 * att_w, :], h)
        w_t = _dot_nt(wq_ref[3 * att_w:3 * att_w + 16, :], h)
    else:
        proj_q = _dot(h, wq_ref[...])

    a = proj_ag[:, 0:c_conv]
    g = proj_ag[:, c_conv:2 * c_conv]
    u = a * jax.nn.sigmoid(g)
    ext_ref[HIST_PAD:HIST_PAD + tm, :] = u
    cw = cw_ref[...]
    cb = cb_ref[...]
    lg = lg_ref[...]
    lb = lb_ref[...]
    lead = HIST_PAD - (CONV_WIDTH - 1)
    rs = min(CONV_ROWS, tm)
    for sh in range(1, 8):
        shift_ref[sh - 1] = ext_ref[sh:sh + tm + HIST_PAD - 8, :]
    for r0 in range(0, tm, rs):
        acc = jnp.broadcast_to(cb, (rs, c_conv))
        for j in range(CONV_WIDTH):
            sh = (j + lead) % 8
            a8 = r0 + (j + lead) // 8 * 8
            if sh == 0:
                tap = ext_ref[a8:a8 + rs, :]
            else:
                tap = shift_ref[sh - 1, a8:a8 + rs, :]
            acc = acc + cw[j:j + 1, :] * tap
        mu = jnp.mean(acc, axis=-1, keepdims=True)
        cen = acc - mu
        var = jnp.mean(cen * cen, axis=-1, keepdims=True)
        y = cen * lax.rsqrt(var + EPS) * lg + lb
        cout_ref[r0:r0 + rs, :] = (y * jax.nn.sigmoid(y)).astype(BF16)

    cs = cs_ref[...]
    sn = sn_ref[...]
    k = _rope_rows(proj_kv[:, 0:att_w], cs, sn)
    v = proj_kv[:, att_w:2 * att_w]
    for hd in range(N_HEADS):
        k_ref[pl.ds(hd, tm, stride=N_HEADS), :] = k[:, hd * HEAD_DIM:(hd + 1) * HEAD_DIM]
        v_ref[pl.ds(hd, tm, stride=N_HEADS), :] = v[:, hd * HEAD_DIM:(hd + 1) * HEAD_DIM]

    ki = _rope_rows(proj_ki, cs, sn)
    kidx_ref[...] = ki[:, :IDX_DIM]

    if transposed:
        kbf_ref, kiext_ref, qt_ref, qit_ref, vt_ref, wto_ref = outs[5:11]
        kbf_ref[...] = k.astype(BF16)
        ki_hi = ki.astype(BF16)
        ki_lo = ki - ki_hi.astype(F32)
        kiext_ref[:, 0:128] = (ki_hi.astype(F32) + pltpu.roll(ki_lo, IDX_DIM, 1)).astype(BF16)
        kiext_ref[:, 128:256] = ki_hi

        ct = ct_ref[...]
        st = st_ref[...]
        half = HEAD_DIM // 2
        zeros_half = jnp.zeros((HEAD_DIM, tm), BF16)
        for hd in range(N_HEADS):
            x1 = q_t[hd * HEAD_DIM:hd * HEAD_DIM + half, :]
            x2 = q_t[hd * HEAD_DIM + half:(hd + 1) * HEAD_DIM, :]
            base = hd * 128 + (hd % 2) * HEAD_DIM
            other = hd * 128 + ((hd + 1) % 2) * HEAD_DIM
            qt_ref[base:base + half, :] = ((x1 * ct - x2 * st) * ATT_SCALE).astype(BF16)
            qt_ref[base + half:base + HEAD_DIM, :] = ((x2 * ct + x1 * st) * ATT_SCALE).astype(BF16)
            qt_ref[other:other + HEAD_DIM, :] = zeros_half
        for hd in range(N_IDX_HEADS):
            r = hd * IDX_DIM
            x1 = qi_t[r:r + half, :]
            x2 = qi_t[r + half:r + IDX_DIM, :]
            y1 = (x1 * ct - x2 * st) * IDX_SCALE
            y2 = (x2 * ct + x1 * st) * IDX_SCALE
            y1h = y1.astype(BF16)
            y2h = y2.astype(BF16)
            y1l = (y1 - y1h.astype(F32)).astype(BF16)
            y2l = (y2 - y2h.astype(F32)).astype(BF16)
            b = hd * 256
            qit_ref[b:b + half, :] = y1h
            qit_ref[b + half:b + 64, :] = y2h
            qit_ref[b + 64:b + 64 + half, :] = y1h
            qit_ref[b + 64 + half:b + 128, :] = y2h
            qit_ref[b + 128:b + 128 + half, :] = y1l
            qit_ref[b + 128 + half:b + 192, :] = y2l
            qit_ref[b + 192:b + 256, :] = zeros_half
        vt_ref[...] = v_t.astype(BF16)
        wto_ref[...] = w_t[0:N_IDX_HEADS, :] * IDX_HEAD_SCALE
    else:
        knew_ref, vnew_ref, q_ref, qi_ref, w_ref = outs[5:10]
        knew_ref[...] = k
        vnew_ref[...] = v
        q_ref[...] = _rope_rows(proj_q[:, 0:att_w], cs, sn) * ATT_SCALE
        qi_ref[...] = _rope_rows(proj_q[:, att_w:2 * att_w], cs, sn) * IDX_SCALE
        w_ref[...] = proj_q[:, 2 * att_w:2 * att_w + 128] * IDX_HEAD_SCALE

    @pl.when(i == n_i - 1)
    def _():
        cstate_ref[...] = ext_ref[tm + lead:tm + HIST_PAD, :]


def _inproj(x, g_pre, w_row, w_q, tabs, hist, conv_w, conv_b, ln_g, ln_b, *, depth, layer, prev, transposed):
    b, t, d = x.shape
    c_conv = conv_w.shape[1]
    att_w = N_HEADS * HEAD_DIM
    tm = min(Q_TILE, t)
    assert t % tm == 0
    nt = t // tm
    cs, sn, ct, st = tabs
    const2 = lambda bi, i: (0, 0)
    row = lambda w: pl.BlockSpec((None, tm, w), lambda bi, i: (bi, i, 0))
    col = lambda r: pl.BlockSpec((None, r, tm), lambda bi, i: (bi, 0, i))
    stacked_shapes = (
        jax.ShapeDtypeStruct((depth, b, t * N_HEADS, HEAD_DIM), F32),
        jax.ShapeDtypeStruct((depth, b, t * N_HEADS, HEAD_DIM), F32),
        jax.ShapeDtypeStruct((depth, b, t, IDX_DIM), F32),
        jax.ShapeDtypeStruct((depth, b, CONV_WIDTH - 1, c_conv), F32),
    )
    stacked_specs = (
        pl.BlockSpec((None, None, tm * N_HEADS, HEAD_DIM), lambda bi, i: (layer, bi, i, 0)),
        pl.BlockSpec((None, None, tm * N_HEADS, HEAD_DIM), lambda bi, i: (layer, bi, i, 0)),
        pl.BlockSpec((None, None, tm, IDX_DIM), lambda bi, i: (layer, bi, i, 0)),
        pl.BlockSpec((None, None, CONV_WIDTH - 1, c_conv), lambda bi, i: (layer, bi, 0, 0)),
    )
    if transposed:
        extra_shapes = (
            jax.ShapeDtypeStruct((b, t, c_conv), BF16),
            jax.ShapeDtypeStruct((b, t, att_w), BF16),
            jax.ShapeDtypeStruct((b, t, 256), BF16),
            jax.ShapeDtypeStruct((b, N_HEADS * 128, t), BF16),
            jax.ShapeDtypeStruct((b, N_IDX_HEADS * 256, t), BF16),
            jax.ShapeDtypeStruct((b, nt, att_w, tm), BF16),
            jax.ShapeDtypeStruct((b, N_IDX_HEADS, t), F32),
        )
        extra_specs = (
            row(c_conv), row(att_w), row(256), col(N_HEADS * 128), col(N_IDX_HEADS * 256),
            pl.BlockSpec((None, None, att_w, tm), lambda bi, i: (bi, i, 0, 0)),
            col(N_IDX_HEADS),
        )
    else:
        extra_shapes = (
            jax.ShapeDtypeStruct((b, t, c_conv), BF16),
            jax.ShapeDtypeStruct((b, t, att_w), F32),
            jax.ShapeDtypeStruct((b, t, att_w), F32),
            jax.ShapeDtypeStruct((b, t, att_w), F32),
            jax.ShapeDtypeStruct((b, t, att_w), F32),
            jax.ShapeDtypeStruct((b, t, 128), F32),
        )
        extra_specs = (row(c_conv), row(att_w), row(att_w), row(att_w), row(att_w), row(128))
    in_specs = [
        pl.BlockSpec((None, tm, d), lambda bi, i: (bi, i, 0)),
        pl.BlockSpec((1, d), const2),
        pl.BlockSpec(w_row.shape, const2, pipeline_mode=pl.Buffered(1)),
        pl.BlockSpec(w_q.shape, const2, pipeline_mode=pl.Buffered(1)),
        pl.BlockSpec((tm, 128), lambda bi, i: (i, 0)),
        pl.BlockSpec((tm, 128), lambda bi, i: (i, 0)),
        pl.BlockSpec((HEAD_DIM // 2, tm), lambda bi, i: (0, i)),
        pl.BlockSpec((HEAD_DIM // 2, tm), lambda bi, i: (0, i)),
        pl.BlockSpec((None, HIST_PAD, c_conv), lambda bi, i: (bi, 0, 0)),
        pl.BlockSpec(conv_w.shape, const2),
        pl.BlockSpec((1, c_conv), const2),
        pl.BlockSpec((1, c_conv), const2),
        pl.BlockSpec((1, c_conv), const2),
    ]
    args = [x, g_pre, w_row, w_q, cs, sn, ct, st, hist, conv_w, conv_b, ln_g, ln_b]
    aliases = {}
    if prev is not None:
        for n, arr in enumerate(prev):
            aliases[len(args)] = n
            in_specs.append(pl.BlockSpec(memory_space=pl.ANY))
            args.append(arr)
    n_alias = 0 if prev is None else len(prev)
    res = pl.pallas_call(
        functools.partial(_inproj_body, tm=tm, c_conv=c_conv, att_w=att_w, transposed=transposed,
                          n_alias=n_alias),
        grid=(b, nt),
        in_specs=in_specs,
        out_specs=stacked_specs + extra_specs,
        out_shape=stacked_shapes + extra_shapes,
        scratch_shapes=[pltpu.VMEM((tm + HIST_PAD, c_conv), F32),
                        pltpu.VMEM((7, tm + HIST_PAD - 8, c_conv), F32)],
        input_output_aliases=aliases,
        compiler_params=pltpu.CompilerParams(
            dimension_semantics=("parallel", "arbitrary"), vmem_limit_bytes=VMEM_LIMIT),
        name="inproj_conv",
    )(*args)
    return res[:4], res[4:]


def _key_to_f32(t):
    return lax.bitcast_convert_type(t ^ (lax.shift_right_arithmetic(t, 31) & 0x7FFFFFFF), F32)


def _seg_loop(segs, body, init):
    carry = init
    for base, n, rows in segs:
        if isinstance(n, int) and n == 1:
            carry = body(base, rows, carry)
        else:
            def step(kc, c, base=base, rows=rows):
                return body(pl.multiple_of(base + kc * rows, rows), rows, c)
            carry = lax.fori_loop(0, n, step, carry)
    return carry


class _ColScores:
    def __init__(self, score_ref, bias_ref, segs, tq):
        self.score_ref, self.bias_ref, self.segs, self.tq = score_ref, bias_ref, segs, tq
        self.vec = (1, tq)

    def count(self, pred):
        tq = self.tq

        def body(r0, rows, acc):
            s = self.score_ref[pl.ds(r0, rows), :]
            m = jnp.where(pred(s), 1, 0).astype(I32)
            return acc + jnp.sum(m.reshape(rows // ACC_ROWS, ACC_ROWS, tq), axis=0)
        acc = _seg_loop(self.segs, body, jnp.zeros((ACC_ROWS, tq), I32))
        return jnp.sum(acc, axis=0, keepdims=True)

    def count_and_max_below(self, bound):
        tq = self.tq

        def body(r0, rows, carry):
            cnt, mx = carry
            s = self.score_ref[pl.ds(r0, rows), :]
            ge = s >= bound
            groups = (rows // ACC_ROWS, ACC_ROWS, tq)
            cnt = cnt + jnp.sum(jnp.where(ge, 1, 0).astype(I32).reshape(groups), axis=0)
            mx = jnp.maximum(mx, jnp.max(jnp.where(ge, -jnp.inf, s).reshape(groups), axis=0))
            return cnt, mx
        cnt, mx = _seg_loop(self.segs, body,
                            (jnp.zeros((ACC_ROWS, tq), I32), jnp.full((ACC_ROWS, tq), -jnp.inf, F32)))
        return jnp.sum(cnt, axis=0, keepdims=True), jnp.max(mx, axis=0, keepdims=True)

    def write_bias(self, gate, tie, need_f):
        def body(r0, rows, seen):
            s = self.score_ref[pl.ds(r0, rows), :]
            if need_f is None:
                tie_bias = 0.0
            else:
                eq = jnp.where(s == tie, 1.0, 0.0).astype(F32)
                ri = lax.broadcasted_iota(I32, (rows, rows), 0)
                ci = lax.broadcasted_iota(I32, (rows, rows), 1)
                lower = jnp.where(ci < ri, 1.0, 0.0).astype(BF16)
                before = _dot(lower, eq.astype(BF16)) + seen
                tie_bias = jnp.where(before < need_f, 0.0, -jnp.inf)
                seen = seen + jnp.sum(eq, axis=0, keepdims=True)
            self.bias_ref[pl.ds(r0, rows), :] = jnp.where(
                s > gate, 0.0, jnp.where(s == tie, tie_bias, -jnp.inf)).astype(F32)
            return seen
        _seg_loop(self.segs, body, jnp.zeros((1, self.tq), F32))


class _RowScores:
    def __init__(self, score_ref, bias_ref, widths):
        self.score_ref, self.bias_ref, self.widths = score_ref, bias_ref, widths
        self.vec = (score_ref.shape[0], 1)

    def count(self, pred):
        s = self.score_ref[...]
        return jnp.sum(jnp.where(pred(s), 1.0, 0.0).astype(F32), axis=1, keepdims=True).astype(I32)

    def count_and_max_below(self, bound):
        s = self.score_ref[...]
        ge = s >= bound
        cnt = jnp.sum(jnp.where(ge, 1.0, 0.0).astype(F32), axis=1, keepdims=True).astype(I32)
        return cnt, jnp.max(jnp.where(ge, -jnp.inf, s), axis=1, keepdims=True)

    def write_bias(self, gate, tie, need_f):
        seen = jnp.zeros(self.vec, F32)
        c0 = 0
        for width in self.widths:
            s = self.score_ref[:, c0:c0 + width]
            if need_f is None:
                tie_bias = 0.0
            else:
                eq = jnp.where(s == tie, 1.0, 0.0).astype(F32)
                ri = lax.broadcasted_iota(I32, (width, width), 0)
                ci = lax.broadcasted_iota(I32, (width, width), 1)
                upper = jnp.where(ri < ci, 1.0, 0.0).astype(BF16)
                before = _dot(eq.astype(BF16), upper) + seen
                tie_bias = jnp.where(before < need_f, 0.0, -jnp.inf)
                seen = seen + jnp.sum(eq, axis=1, keepdims=True)
            self.bias_ref[:, c0:c0 + width] = jnp.where(
                s > gate, 0.0, jnp.where(s == tie, tie_bias, -jnp.inf)).astype(F32)
            c0 += width


def _select_bias(sc, topk, n_adm):
    c0 = sc.count(lambda s: s >= 0.0)
    nonneg = c0 >= topk
    p = jnp.where(nonneg, 0, INT_MIN).astype(I32)
    c_p = jnp.where(nonneg, c0, 0).astype(I32)

    def bit_body(b, carry):
        p, c_p = carry
        t = p + lax.shift_left(jnp.int32(1), 30 - b)
        tf = _key_to_f32(t)
        c = sc.count(lambda s: s >= tf)
        take = c >= topk
        return jnp.where(take, t, p), jnp.where(take, c, c_p)

    p, c_p = lax.fori_loop(0, 31, bit_body, (p, c_p))
    p_f = _key_to_f32(p)
    c_above, v0 = sc.count_and_max_below(_key_to_f32(p + 1))
    take_all = n_adm <= topk
    settled = take_all | (v0 == p_f)
    fin0 = jnp.where(settled, 1, 0).astype(I32)

    def walk_cond(st):
        return st[0] > 0

    def walk_body(st):
        _, v, cnt_above, thr, c_thr, n_gt, fin = st
        c_ge, nxt = sc.count_and_max_below(v)
        ok = c_ge >= topk
        newly = ok & (fin == 0)
        thr = jnp.where(newly, v, thr)
        c_thr = jnp.where(newly, c_ge, c_thr)
        n_gt = jnp.where(newly, cnt_above, n_gt)
        fin = jnp.where(ok, 1, fin)
        cnt_above = jnp.where(fin > 0, cnt_above, c_ge)
        v = jnp.where(fin > 0, v, nxt)
        return jnp.sum(1 - fin), v, cnt_above, thr, c_thr, n_gt, fin

    init = (jnp.sum(1 - fin0), v0, c_above, v0, c_p, c_above, fin0)
    _, _, _, thr, c_thr, n_gt, _ = lax.while_loop(walk_cond, walk_body, init)

    need = topk - n_gt
    gate = jnp.where(take_all, -jnp.inf, thr)
    tie = jnp.where(take_all, jnp.inf, thr)
    ambiguous = jnp.logical_not(take_all) & ((c_thr - n_gt) > need)
    n_amb = jnp.sum(jnp.where(ambiguous, 1, 0).astype(I32))

    @pl.when(n_amb == 0)
    def _():
        sc.write_bias(gate, tie, None)

    @pl.when(n_amb > 0)
    def _():
        sc.write_bias(gate, tie, need.astype(F32))


def _admissible_count(qpos, s_total):
    return jnp.minimum((lax.shift_right_arithmetic(qpos, CHUNK_SHIFT) + 1) * (1 << CHUNK_SHIFT), s_total)


def _attn_prompt_body(kbf_ref, kiext_ref, vt_ref, qt_ref, qit_ref, wt_ref, out_ref,
                      score_ref, bias_ref, m_ref, l_ref, o_ref, att_ref, *, tq, topk, s_total):
    j = pl.program_id(1)
    segs = [(0, j + 1, KEY_CHUNK)]
    w = wt_ref[...]
    qpos = j * tq + lax.broadcasted_iota(I32, (1, tq), 1)
    qchunk = lax.shift_right_arithmetic(qpos, CHUNK_SHIFT)

    def score_body(r0, rows, c):
        kie = kiext_ref[pl.ds(r0, rows), :]
        s = jnp.zeros((rows, tq), F32)
        for hd in range(N_IDX_HEADS):
            lgt = _dot(kie, qit_ref[hd * 256:(hd + 1) * 256, :])
            s = s + jnp.maximum(lgt, 0.0) * w[hd:hd + 1, :]
        kpos = r0 + lax.broadcasted_iota(I32, (rows, 1), 0)
        adm = lax.shift_right_arithmetic(kpos, CHUNK_SHIFT) <= qchunk
        score_ref[pl.ds(r0, rows), :] = jnp.where(adm, s, -jnp.inf)
        return c

    _seg_loop(segs, score_body, 0)
    _select_bias(_ColScores(score_ref, bias_ref, segs, tq), topk, _admissible_count(qpos, s_total))

    m_ref[...] = jnp.full(m_ref.shape, -jnp.inf, F32)
    l_ref[...] = jnp.zeros(l_ref.shape, F32)
    o_ref[...] = jnp.zeros(o_ref.shape, F32)

    def qk_body(kc, c):
        r0 = pl.multiple_of(kc * KEY_CHUNK, KEY_CHUNK)
        bias = bias_ref[pl.ds(r0, KEY_CHUNK), :]
        for hd in range(N_HEADS):
            pair = hd // 2
            kb = kbf_ref[pl.ds(r0, KEY_CHUNK), pair * 128:(pair + 1) * 128]
            a = _dot(kb, qt_ref[hd * 128:(hd + 1) * 128, :]) + bias
            att_ref[hd, pl.ds(r0, KEY_CHUNK), :] = a
            ms = slice(hd * 8, (hd + 1) * 8)
            m_ref[ms, :] = jnp.maximum(m_ref[ms, :], jnp.max(a.reshape(KEY_CHUNK // 8, 8, tq), axis=0))
        return c

    lax.fori_loop(0, j + 1, qk_body, 0)
    for hd in range(N_HEADS):
        ms = slice(hd * 8, (hd + 1) * 8)
        m_ref[ms, :] = jnp.broadcast_to(jnp.max(m_ref[ms, :], axis=0, keepdims=True), (8, tq))

    def pv_body(kc, c):
        r0 = pl.multiple_of(kc * KEY_CHUNK, KEY_CHUNK)
        for hd in range(N_HEADS):
            ms = slice(hd * 8, (hd + 1) * 8)
            hs = slice(hd * HEAD_DIM, (hd + 1) * HEAD_DIM)
            p = jnp.exp(att_ref[hd, pl.ds(r0, KEY_CHUNK), :] - m_ref[hd * 8:hd * 8 + 1, :])
            l_ref[ms, :] = l_ref[ms, :] + jnp.sum(p.reshape(KEY_CHUNK // 8, 8, tq), axis=0)
            o_ref[hs, :] = o_ref[hs, :] + _dot(vt_ref[kc, hs, :], p.astype(BF16))
        return c

    lax.fori_loop(0, j + 1, pv_body, 0)
    for hd in range(N_HEADS):
        hs = slice(hd * HEAD_DIM, (hd + 1) * HEAD_DIM)
        l = jnp.sum(l_ref[hd * 8:(hd + 1) * 8, :], axis=0, keepdims=True)
        out_ref[hs, :] = (o_ref[hs, :] / l).astype(BF16)


def _attn_prompt(kbf, kiext, vt, qt, qit, wt):
    b, t, att_w = kbf.shape
    tq = Q_TILE
    assert t % tq == 0 and tq == KEY_CHUNK and vt.shape[3] == KEY_CHUNK
    nq = t // tq
    topk = min(MAX_TOPK, t // 4)
    full = lambda shape: pl.BlockSpec((None,) + shape, lambda bi, j: (bi,) + (0,) * len(shape))
    col = lambda r: pl.BlockSpec((None, r, tq), lambda bi, j: (bi, 0, j))
    return pl.pallas_call(
        functools.partial(_attn_prompt_body, tq=tq, topk=topk, s_total=t),
        grid=(b, nq),
        in_specs=[full((t, att_w)), full((t, 256)), full((nq, att_w, KEY_CHUNK)),
                  col(N_HEADS * 128), col(N_IDX_HEADS * 256), col(N_IDX_HEADS)],
        out_specs=col(att_w),
        out_shape=jax.ShapeDtypeStruct((b, att_w, t), BF16),
        scratch_shapes=[pltpu.VMEM((t, tq), F32), pltpu.VMEM((t, tq), F32),
                        pltpu.VMEM((N_HEADS * 8, tq), F32), pltpu.VMEM((N_HEADS * 8, tq), F32),
                        pltpu.VMEM((att_w, tq), F32), pltpu.VMEM((N_HEADS, t, tq), F32)],
        compiler_params=pltpu.CompilerParams(
            dimension_semantics=("parallel", "arbitrary"), vmem_limit_bytes=VMEM_LIMIT),
        name="attn_prompt",
    )(kbf, kiext, vt, qt, qit, wt)


def _attn_sample_body(ck_ref, cv_ref, cki_ref, kn_ref, vn_ref, kin_ref, q_ref, qi_ref, w_ref, out_ref,
                      score_ref, bias_ref, att_ref, o_ref, *, s_past, ts, topk):
    s_total = s_past + ts
    rows = N_HEADS * ts
    blocks = [(c0, KEY_CHUNK) for c0 in range(0, s_past, KEY_CHUNK)] + [(s_past, NEW_KEY_PAD)]

    def key_block(ref_past, ref_new, c0, width):
        return ref_past[c0:c0 + width, :] if c0 < s_past else ref_new[...]

    def split(x):
        hi = x.astype(BF16)
        return hi, (x - hi.astype(F32)).astype(BF16)

    qi = qi_ref[...]
    qi_hi, qi_lo = split(jnp.concatenate(
        [qi[:, hd * IDX_DIM:(hd + 1) * IDX_DIM] for hd in range(N_IDX_HEADS)], axis=0))
    w = w_ref[...]
    qpos = s_past + lax.broadcasted_iota(I32, (ts, 1), 0)
    qchunk = lax.shift_right_arithmetic(qpos, CHUNK_SHIFT)
    for c0, width in blocks:
        ki_hi, ki_lo = split(key_block(cki_ref, kin_ref, c0, width))
        lgt = _dot_nt(qi_hi, ki_hi) + _dot_nt(qi_hi, ki_lo) + _dot_nt(qi_lo, ki_hi)
        s = jnp.zeros((ts, width), F32)
        for hd in range(N_IDX_HEADS):
            s = s + jnp.maximum(lgt[hd * ts:(hd + 1) * ts, :], 0.0) * w[:, hd:hd + 1]
        kpos = c0 + lax.broadcasted_iota(I32, (1, width), 1)
        adm = lax.shift_right_arithmetic(kpos, CHUNK_SHIFT) <= qchunk
        s = jnp.where(kpos < s_total, jnp.where(adm, s, -jnp.inf), -jnp.inf)
        score_ref[:, c0:c0 + width] = s

    _select_bias(_RowScores(score_ref, bias_ref, [wd for _, wd in blocks]), topk,
                 _admissible_count(qpos, s_total))

    q = q_ref[...]
    head_of_lane = lax.shift_right_arithmetic(lax.broadcasted_iota(I32, (ts, N_HEADS * HEAD_DIM), 1), HEAD_SHIFT)
    q_bd = jnp.concatenate([jnp.where(head_of_lane == hd, q, 0.0) for hd in range(N_HEADS)],
                           axis=0).astype(BF16)
    m_acc = jnp.full((rows, 128), -jnp.inf, F32)
    for c0, width in blocks:
        kb = key_block(ck_ref, kn_ref, c0, width).astype(BF16)
        a = _dot_nt(q_bd, kb)
        bias = bias_ref[:, c0:c0 + width]
        a = a + jnp.concatenate([bias] * N_HEADS, axis=0)
        att_ref[:, c0:c0 + width] = a
        for l0 in range(0, width, 128):
            m_acc = jnp.maximum(m_acc, a[:, l0:l0 + 128])
    m = jnp.max(m_acc, axis=1, keepdims=True)
    l_acc = jnp.zeros((rows, 128), F32)
    o_ref[...] = jnp.zeros(o_ref.shape, F32)
    for c0, width in blocks:
        p = jnp.exp(att_ref[:, c0:c0 + width] - m)
        for l0 in range(0, width, 128):
            l_acc = l_acc + p[:, l0:l0 + 128]
        vb = key_block(cv_ref, vn_ref, c0, width).astype(BF16)
        o_ref[...] = o_ref[...] + _dot(p.astype(BF16), vb)
    o = o_ref[...] / jnp.sum(l_acc, axis=1, keepdims=True)
    out = jnp.zeros((ts, N_HEADS * HEAD_DIM), F32)
    for hd in range(N_HEADS):
        out = out + jnp.where(head_of_lane == hd, o[hd * ts:(hd + 1) * ts, :], 0.0)
    out_ref[...] = out.astype(BF16)


def _attn_sample(cache_k, cache_v, cache_kidx, k_new, v_new, ki_new, q, qi, w, *, layer):
    _, b, s_past, att_w = cache_k.shape
    ts = q.shape[1]
    assert s_past % KEY_CHUNK == 0 and ts % 8 == 0 and ts <= NEW_KEY_PAD
    assert k_new.shape[1] == NEW_KEY_PAD
    s_pad = s_past + NEW_KEY_PAD
    topk = min(MAX_TOPK, (s_past + ts) // 4)
    full = lambda shape: pl.BlockSpec((None,) + shape, lambda bi: (bi,) + (0,) * len(shape))
    cached = lambda width: pl.BlockSpec((None, None, s_past, width), lambda bi: (layer, bi, 0, 0))
    return pl.pallas_call(
        functools.partial(_attn_sample_body, s_past=s_past, ts=ts, topk=topk),
        grid=(b,),
        in_specs=[cached(att_w), cached(att_w), cached(IDX_DIM),
                  full((NEW_KEY_PAD, att_w)), full((NEW_KEY_PAD, att_w)), full((NEW_KEY_PAD, IDX_DIM)),
                  full((ts, att_w)), full((ts, att_w)), full((ts, 128))],
        out_specs=full((ts, att_w)),
        out_shape=jax.ShapeDtypeStruct((b, ts, att_w), BF16),
        scratch_shapes=[pltpu.VMEM((ts, s_pad), F32), pltpu.VMEM((ts, s_pad), F32),
                        pltpu.VMEM((N_HEADS * ts, s_pad), F32), pltpu.VMEM((N_HEADS * ts, att_w), F32)],
        compiler_params=pltpu.CompilerParams(
            dimension_semantics=("parallel",), vmem_limit_bytes=VMEM_LIMIT),
        name="attn_sample",
    )(cache_k, cache_v, cache_kidx, k_new, v_new, ki_new, q, qi, w)


def _outproj_ffn_body(x_ref, cout_ref, att_ref, wc_ref, wa_ref, g_ref, gpre_ref, gpost_ref, wup_ref, wdown_ref,
                      o_ref, *, att_transposed, d_ff):
    m = _dot(cout_ref[...], wc_ref[...])
    if att_transposed:
        m = m + _dot_tn(att_ref[...], wa_ref[...])
    else:
        m = m + _dot(att_ref[...], wa_ref[...])
    x = x_ref[...] + _rms(m, g_ref[...])
    o_ref[...] = _half_step_ffn(x, gpre_ref[...], gpost_ref[...], wup_ref, wdown_ref, d_ff)


def _outproj_ffn(x, cout, att, w_conv, w_att, g_post, att_transposed, g_ffn_pre, g_ffn_post, w_up, w_down):
    b, t, d = x.shape
    c_conv = cout.shape[2]
    att_w = w_att.shape[0]
    d_ff = w_down.shape[0]
    tm = min(512, t)
    assert t % tm == 0
    const2 = lambda bi, i: (0, 0)
    if att_transposed:
        att_spec = pl.BlockSpec((None, att_w, tm), lambda bi, i: (bi, 0, i))
    else:
        att_spec = pl.BlockSpec((None, tm, att_w), lambda bi, i: (bi, i, 0))
    return pl.pallas_call(
        functools.partial(_outproj_ffn_body, att_transposed=att_transposed, d_ff=d_ff),
        grid=(b, t // tm),
        in_specs=[
            pl.BlockSpec((None, tm, d), lambda bi, i: (bi, i, 0)),
            pl.BlockSpec((None, tm, c_conv), lambda bi, i: (bi, i, 0)),
            att_spec,
            pl.BlockSpec(w_conv.shape, const2, pipeline_mode=pl.Buffered(1)),
            pl.BlockSpec(w_att.shape, const2, pipeline_mode=pl.Buffered(1)),
            pl.BlockSpec((1, d), const2),
            pl.BlockSpec((1, d), const2),
            pl.BlockSpec((1, d), const2),
            pl.BlockSpec((d, 2 * d_ff), const2, pipeline_mode=pl.Buffered(1)),
            pl.BlockSpec((d_ff, d), const2, pipeline_mode=pl.Buffered(1)),
        ],
        out_specs=pl.BlockSpec((None, tm, d), lambda bi, i: (bi, i, 0)),
        out_shape=jax.ShapeDtypeStruct((b, t, d), F32),
        compiler_params=pltpu.CompilerParams(
            dimension_semantics=("parallel", "parallel"), vmem_limit_bytes=VMEM_LIMIT),
        name="outproj_ffn",
    )(x, cout, att, w_conv, w_att, g_post, g_ffn_pre, g_ffn_post, w_up, w_down)


def _rope_tables(pos):
    inv = 1.0 / (ROPE_THETA ** (jnp.arange(0, HEAD_DIM, 2, dtype=F32) / HEAD_DIM))
    ang = pos.astype(F32)[:, None] * inv[None, :]
    cos, sin = jnp.cos(ang), jnp.sin(ang)
    cs = jnp.concatenate([cos, cos, cos, cos], axis=1)
    sn = jnp.concatenate([-sin, sin, -sin, sin], axis=1)
    return cs, sn, cos.T, sin.T


def _ffn_tokens(x, g_pre, g_post, w_up, w_down):
    b, t, d = x.shape
    return _ffn(x.reshape(b * t, d), g_pre, g_post, w_up, w_down).reshape(b, t, d)


def kernel(x_prompt, x_sample, cache_k, cache_v, cache_kidx, state_conv, ffn1_norm_pre, ffn1_norm_post, ffn1_w_up, ffn1_w_down, mix_norm_pre, mix_norm_post, w_in, conv_w, conv_b, conv_ln_g, conv_ln_b, w_out, ffn2_norm_pre, ffn2_norm_post, ffn2_w_up, ffn2_w_down):
    depth = w_in.shape[0]
    d = x_prompt.shape[2]
    c_conv = conv_w.shape[2]
    att_w = N_HEADS * HEAD_DIM
    assert IDX_DIM == HEAD_DIM and N_IDX_HEADS == N_HEADS
    assert w_in.shape[2] == 2 * c_conv + 3 * att_w + N_IDX_HEADS * IDX_DIM + IDX_DIM + N_IDX_HEADS
    bp, tp, _ = x_prompt.shape
    bs, ts, _ = x_sample.shape
    past_len = cache_k.shape[2]

    tabs_p = _rope_tables(jnp.arange(tp))
    tabs_s = _rope_tables(past_len + jnp.arange(ts))
    hist_p = jnp.zeros((bp, HIST_PAD, c_conv), F32)
    lead = HIST_PAD - (CONV_WIDTH - 1)
    pad_new = lambda a: jnp.pad(a, ((0, 0), (0, NEW_KEY_PAD - ts), (0, 0)))
    ck_rows = cache_k.reshape(depth, bs, past_len, att_w).astype(BF16)
    cv_rows = cache_v.reshape(depth, bs, past_len, att_w).astype(BF16)

    yp, ys = x_prompt, x_sample
    stack_p = stack_s = None
    for l in range(depth):
        vec = lambda a: a[l][None, :]
        o_q = 2 * c_conv
        o_k = o_q + att_w
        o_v = o_k + att_w
        o_qi = o_v + att_w
        o_ki = o_qi + N_IDX_HEADS * IDX_DIM
        o_wi = o_ki + IDX_DIM
        wl = w_in[l]
        w_row = jnp.concatenate(
            [wl[:, :o_q], wl[:, o_k:o_qi], wl[:, o_ki:o_wi], jnp.zeros((d, 128 - IDX_DIM), F32)],
            axis=1).astype(BF16)
        w_qt = jnp.concatenate(
            [wl[:, o_q:o_k], wl[:, o_qi:o_ki], wl[:, o_v:o_qi], wl[:, o_wi:],
             jnp.zeros((d, 16 - N_IDX_HEADS), F32)], axis=1).T.astype(BF16)
        w_qrow = jnp.concatenate(
            [wl[:, o_q:o_k], wl[:, o_qi:o_ki], wl[:, o_wi:], jnp.zeros((d, 128 - N_IDX_HEADS), F32)],
            axis=1).astype(BF16)
        w1u, w1d = ffn1_w_up[l].astype(BF16), ffn1_w_down[l].astype(BF16)
        w2u, w2d = ffn2_w_up[l].astype(BF16), ffn2_w_down[l].astype(BF16)
        wo_c, wo_a = w_out[l, :c_conv].astype(BF16), w_out[l, c_conv:].astype(BF16)
        hist_s = jnp.pad(state_conv[l], ((0, 0), (lead, 0), (0, 0)))

        yp = _ffn_tokens(yp, vec(ffn1_norm_pre), vec(ffn1_norm_post), w1u, w1d)
        ys = _ffn_tokens(ys, vec(ffn1_norm_pre), vec(ffn1_norm_post), w1u, w1d)

        conv_args = (conv_w[l], vec(conv_b), vec(conv_ln_g), vec(conv_ln_b))
        stack_p, (cout_p, kbf_p, kiext_p, qt_p, qit_p, vt_p, wt_p) = _inproj(
            yp, vec(mix_norm_pre), w_row, w_qt, tabs_p, hist_p, *conv_args,
            depth=depth, layer=l, prev=stack_p, transposed=True)
        stack_s, (cout_s, knew_s, vnew_s, q_s, qi_s, w_s) = _inproj(
            ys, vec(mix_norm_pre), w_row, w_qrow, tabs_s, hist_s, *conv_args,
            depth=depth, layer=l, prev=stack_s, transposed=False)

        att_p = _attn_prompt(kbf_p, kiext_p, vt_p, qt_p, qit_p, wt_p)
        att_s = _attn_sample(ck_rows, cv_rows, cache_kidx, pad_new(knew_s), pad_new(vnew_s),
                             pad_new(stack_s[2][l]), q_s, qi_s, w_s, layer=l)

        ffn2 = (vec(ffn2_norm_pre), vec(ffn2_norm_post), w2u, w2d)
        yp = _outproj_ffn(yp, cout_p, att_p, wo_c, wo_a, vec(mix_norm_post), True, *ffn2)
        flat = lambda a: a.reshape(1, bs * ts, a.shape[2])
        ys = _outproj_ffn(flat(ys), flat(cout_s), flat(att_s), wo_c, wo_a, vec(mix_norm_post), False,
                          *ffn2).reshape(bs, ts, d)

    def heads(stack, b, t):
        k, v, kidx, cstate = stack
        shape = (depth, b, t, N_HEADS, HEAD_DIM)
        return k.reshape(shape), v.reshape(shape), kidx, cstate

    return (yp, ys) + heads(stack_p, bp, tp) + heads(stack_s, bs, ts)
```

```python
import functools

import jax
import jax.numpy as jnp
from jax import lax
from jax.experimental import pallas as pl
from jax.experimental.pallas import tpu as pltpu

F32 = jnp.float32
BF16 = jnp.bfloat16
I32 = jnp.int32

CHUNK_SHIFT = 6
CONV_WIDTH = 31
N_HEADS = 8
HEAD_DIM = 64
HEAD_SHIFT = 6
N_IDX_HEADS = 8
IDX_DIM = 64
MAX_TOPK = 256
ROPE_THETA = 10000.0
EPS = 1e-6
ATT_SCALE = HEAD_DIM ** -0.5
IDX_SCALE = IDX_DIM ** -0.5
IDX_HEAD_SCALE = N_IDX_HEADS ** -0.5

KEY_CHUNK = 256
Q_TILE = 256
NEW_KEY_PAD = 128
HIST_PAD = 32
CONV_ROWS = 32
FF_CHUNK = 512
VMEM_LIMIT = 56 * 1024 * 1024

ACC_ROWS = 32
INT_MIN = -(2 ** 31)


def _rms(x, g):
    return x * lax.rsqrt(jnp.mean(x * x, axis=-1, keepdims=True) + EPS) * g


def _dot(a, b):
    return jnp.dot(a, b, preferred_element_type=F32)


def _dot_nt(a, b):
    return lax.dot_general(a, b, (((1,), (1,)), ((), ())), preferred_element_type=F32)


def _dot_tn(a, b):
    return lax.dot_general(a, b, (((0,), (0,)), ((), ())), preferred_element_type=F32)


def _half_step_ffn(x, g_pre, g_post, wup_ref, wdown_ref, d_ff):
    xn = _rms(x, g_pre).astype(BF16)
    acc = None
    for c0 in range(0, d_ff, FF_CHUNK):
        c1 = min(c0 + FF_CHUNK, d_ff)
        gate = _dot(xn, wup_ref[:, c0:c1])
        up = _dot(xn, wup_ref[:, d_ff + c0:d_ff + c1])
        act = (gate * jax.nn.sigmoid(gate) * up).astype(BF16)
        part = _dot(act, wdown_ref[c0:c1, :])
        acc = part if acc is None else acc + part
    return x + 0.5 * _rms(acc, g_post)


def _ffn_body(x_ref, gpre_ref, gpost_ref, wup_ref, wdown_ref, o_ref, *, d_ff):
    o_ref[...] = _half_step_ffn(x_ref[...], gpre_ref[...], gpost_ref[...], wup_ref, wdown_ref, d_ff)


def _ffn(x, g_pre, g_post, w_up, w_down):
    n, d = x.shape
    d_ff = w_down.shape[0]
    tm = min(512, n)
    assert n % tm == 0
    const = lambda i: (0, 0)
    return pl.pallas_call(
        functools.partial(_ffn_body, d_ff=d_ff),
        grid=(n // tm,),
        in_specs=[
            pl.BlockSpec((tm, d), lambda i: (i, 0)),
            pl.BlockSpec((1, d), const),
            pl.BlockSpec((1, d), const),
            pl.BlockSpec((d, 2 * d_ff), const, pipeline_mode=pl.Buffered(1)),
            pl.BlockSpec((d_ff, d), const, pipeline_mode=pl.Buffered(1)),
        ],
        out_specs=pl.BlockSpec((tm, d), lambda i: (i, 0)),
        out_shape=jax.ShapeDtypeStruct((n, d), F32),
        compiler_params=pltpu.CompilerParams(
            dimension_semantics=("parallel",), vmem_limit_bytes=VMEM_LIMIT),
        name="ffn",
    )(x, g_pre, g_post, w_up, w_down)


def _rope_rows(x, cs, sn):
    tm, width = x.shape
    lane = lax.broadcasted_iota(I32, (tm, width), 1)
    first_half = (lane & (HEAD_DIM - 1)) < (HEAD_DIM // 2)
    rot = jnp.where(first_half, pltpu.roll(x, width - HEAD_DIM // 2, 1), pltpu.roll(x, HEAD_DIM // 2, 1))
    reps = width // 128
    return x * jnp.concatenate([cs] * reps, axis=1) + rot * jnp.concatenate([sn] * reps, axis=1)


def _inproj_body(*refs, tm, c_conv, att_w, transposed, n_alias):
    (x_ref, g_ref, wrow_ref, wq_ref, cs_ref, sn_ref, ct_ref, st_ref, hist_ref,
     cw_ref, cb_ref, lg_ref, lb_ref) = refs[:13]
    outs = refs[13 + n_alias:]
    k_ref, v_ref, kidx_ref, cstate_ref, cout_ref = outs[:5]
    ext_ref, shift_ref = outs[-2:]
    i = pl.program_id(1)
    n_i = pl.num_programs(1)
    @pl.when(i == 0)
    def _():
        ext_ref[0:HIST_PAD, :] = hist_ref[...]

    @pl.when(i > 0)
    def _():
        ext_ref[0:HIST_PAD, :] = ext_ref[tm:tm + HIST_PAD, :]

    h = _rms(x_ref[...], g_ref[...]).astype(BF16)
    k0 = 2 * c_conv
    proj_ag = _dot(h, wrow_ref[:, 0:k0])
    proj_kv = _dot(h, wrow_ref[:, k0:k0 + 2 * att_w])
    proj_ki = _dot(h, wrow_ref[:, k0 + 2 * att_w:k0 + 2 * att_w + 128])
    if transposed:
        q_t = _dot_nt(wq_ref[0:att_w, :], h)
        qi_t = _dot_nt(wq_ref[att_w:2 * att_w, :], h)
        v_t = _dot_nt(wq_ref[2 * att_w:3 * att_w, :], h)
        w_t = _dot_nt(wq_ref[3 * att_w:3 * att_w + 16, :], h)
    else:
        proj_q = _dot(h, wq_ref[...])

    a = proj_ag[:, 0:c_conv]
    g = proj_ag[:, c_conv:2 * c_conv]
    u = a * jax.nn.sigmoid(g)
    ext_ref[HIST_PAD:HIST_PAD + tm, :] = u
    cw = cw_ref[...]
    cb = cb_ref[...]
    lg = lg_ref[...]
    lb = lb_ref[...]
    lead = HIST_PAD - (CONV_WIDTH - 1)
    rs = min(CONV_ROWS, tm)
    for sh in range(1, 8):
        shift_ref[sh - 1] = ext_ref[sh:sh + tm + HIST_PAD - 8, :]
    for r0 in range(0, tm, rs):
        acc = jnp.broadcast_to(cb, (rs, c_conv))
        for j in range(CONV_WIDTH):
            sh = (j + lead) % 8
            a8 = r0 + (j + lead) // 8 * 8
            if sh == 0:
                tap = ext_ref[a8:a8 + rs, :]
            else:
                tap = shift_ref[sh - 1, a8:a8 + rs, :]
            acc = acc + cw[j:j + 1, :] * tap
        mu = jnp.mean(acc, axis=-1, keepdims=True)
        cen = acc - mu
        var = jnp.mean(cen * cen, axis=-1, keepdims=True)
        y = cen * lax.rsqrt(var + EPS) * lg + lb
        cout_ref[r0:r0 + rs, :] = (y * jax.nn.sigmoid(y)).astype(BF16)

    cs = cs_ref[...]
    sn = sn_ref[...]
    k = _rope_rows(proj_kv[:, 0:att_w], cs, sn)
    v = proj_kv[:, att_w:2 * att_w]
    for hd in range(N_HEADS):
        k_ref[pl.ds(hd, tm, stride=N_HEADS), :] = k[:, hd * HEAD_DIM:(hd + 1) * HEAD_DIM]
        v_ref[pl.ds(hd, tm, stride=N_HEADS), :] = v[:, hd * HEAD_DIM:(hd + 1) * HEAD_DIM]

    ki = _rope_rows(proj_ki, cs, sn)
    kidx_ref[...] = ki[:, :IDX_DIM]

    if transposed:
        kbf_ref, kiext_ref, qt_ref, qit_ref, vt_ref, wto_ref = outs[5:11]
        kbf_ref[...] = k.astype(BF16)
        ki_hi = ki.astype(BF16)
        ki_lo = ki - ki_hi.astype(F32)
        kiext_ref[:, 0:128] = (ki_hi.astype(F32) + pltpu.roll(ki_lo, IDX_DIM, 1)).astype(BF16)
        kiext_ref[:, 128:256] = ki_hi

        ct = ct_ref[...]
        st = st_ref[...]
        half = HEAD_DIM // 2
        zeros_half = jnp.zeros((HEAD_DIM, tm), BF16)
        for hd in range(N_HEADS):
            x1 = q_t[hd * HEAD_DIM:hd * HEAD_DIM + half, :]
            x2 = q_t[hd * HEAD_DIM + half:(hd + 1) * HEAD_DIM, :]
            base = hd * 128 + (hd % 2) * HEAD_DIM
            other = hd * 128 + ((hd + 1) % 2) * HEAD_DIM
            qt_ref[base:base + half, :] = ((x1 * ct - x2 * st) * ATT_SCALE).astype(BF16)
            qt_ref[base + half:base + HEAD_DIM, :] = ((x2 * ct + x1 * st) * ATT_SCALE).astype(BF16)
            qt_ref[other:other + HEAD_DIM, :] = zeros_half
        for hd in range(N_IDX_HEADS):
            r = hd * IDX_DIM
            x1 = qi_t[r:r + half, :]
            x2 = qi_t[r + half:r + IDX_DIM, :]
            y1 = (x1 * ct - x2 * st) * IDX_SCALE
            y2 = (x2 * ct + x1 * st) * IDX_SCALE
            y1h = y1.astype(BF16)
            y2h = y2.astype(BF16)
            y1l = (y1 - y1h.astype(F32)).astype(BF16)
            y2l = (y2 - y2h.astype(F32)).astype(BF16)
            b = hd * 256
            qit_ref[b:b + half, :] = y1h
            qit_ref[b + half:b + 64, :] = y2h
            qit_ref[b + 64:b + 64 + half, :] = y1h
            qit_ref[b + 64 + half:b + 128, :] = y2h
            qit_ref[b + 128:b + 128 + half, :] = y1l
            qit_ref[b + 128 + half:b + 192, :] = y2l
            qit_ref[b + 192:b + 256, :] = zeros_half
        vt_ref[...] = v_t.astype(BF16)
        wto_ref[...] = w_t[0:N_IDX_HEADS, :] * IDX_HEAD_SCALE
    else:
        knew_ref, vnew_ref, q_ref, qi_ref, w_ref = outs[5:10]
        knew_ref[...] = k
        vnew_ref[...] = v
        q_ref[...] = _rope_rows(proj_q[:, 0:att_w], cs, sn) * ATT_SCALE
        qi_ref[...] = _rope_rows(proj_q[:, att_w:2 * att_w], cs, sn) * IDX_SCALE
        w_ref[...] = proj_q[:, 2 * att_w:2 * att_w + 128] * IDX_HEAD_SCALE

    @pl.when(i == n_i - 1)
    def _():
        cstate_ref[...] = ext_ref[tm + lead:tm + HIST_PAD, :]


def _inproj(x, g_pre, w_row, w_q, tabs, hist, conv_w, conv_b, ln_g, ln_b, *, depth, layer, prev, transposed):
    b, t, d = x.shape
    c_conv = conv_w.shape[1]
    att_w = N_HEADS * HEAD_DIM
    tm = min(Q_TILE, t)
    assert t % tm == 0
    nt = t // tm
    cs, sn, ct, st = tabs
    const2 = lambda bi, i: (0, 0)
    row = lambda w: pl.BlockSpec((None, tm, w), lambda bi, i: (bi, i, 0))
    col = lambda r: pl.BlockSpec((None, r, tm), lambda bi, i: (bi, 0, i))
    stacked_shapes = (
        jax.ShapeDtypeStruct((depth, b, t * N_HEADS, HEAD_DIM), F32),
        jax.ShapeDtypeStruct((depth, b, t * N_HEADS, HEAD_DIM), F32),
        jax.ShapeDtypeStruct((depth, b, t, IDX_DIM), F32),
        jax.ShapeDtypeStruct((depth, b, CONV_WIDTH - 1, c_conv), F32),
    )
    stacked_specs = (
        pl.BlockSpec((None, None, tm * N_HEADS, HEAD_DIM), lambda bi, i: (layer, bi, i, 0)),
        pl.BlockSpec((None, None, tm * N_HEADS, HEAD_DIM), lambda bi, i: (layer, bi, i, 0)),
        pl.BlockSpec((None, None, tm, IDX_DIM), lambda bi, i: (layer, bi, i, 0)),
        pl.BlockSpec((None, None, CONV_WIDTH - 1, c_conv), lambda bi, i: (layer, bi, 0, 0)),
    )
    if transposed:
        extra_shapes = (
            jax.ShapeDtypeStruct((b, t, c_conv), BF16),
            jax.ShapeDtypeStruct((b, t, att_w), BF16),
            jax.ShapeDtypeStruct((b, t, 256), BF16),
            jax.ShapeDtypeStruct((b, N_HEADS * 128, t), BF16),
            jax.ShapeDtypeStruct((b, N_IDX_HEADS * 256, t), BF16),
            jax.ShapeDtypeStruct((b, nt, att_w, tm), BF16),
            jax.ShapeDtypeStruct((b, N_IDX_HEADS, t), F32),
        )
        extra_specs = (
            row(c_conv), row(att_w), row(256), col(N_HEADS * 128), col(N_IDX_HEADS * 256),
            pl.BlockSpec((None, None, att_w, tm), lambda bi, i: (bi, i, 0, 0)),
            col(N_IDX_HEADS),
        )
    else:
        extra_shapes = (
            jax.ShapeDtypeStruct((b, t, c_conv), BF16),
            jax.ShapeDtypeStruct((b, t, att_w), F32),
            jax.ShapeDtypeStruct((b, t, att_w), F32),
            jax.ShapeDtypeStruct((b, t, att_w), F32),
            jax.ShapeDtypeStruct((b, t, att_w), F32),
            jax.ShapeDtypeStruct((b, t, 128), F32),
        )
        extra_specs = (row(c_conv), row(att_w), row(att_w), row(att_w), row(att_w), row(128))
    in_specs = [
        pl.BlockSpec((None, tm, d), lambda bi, i: (bi, i, 0)),
        pl.BlockSpec((1, d), const2),
        pl.BlockSpec(w_row.shape, const2, pipeline_mode=pl.Buffered(1)),
        pl.BlockSpec(w_q.shape, const2, pipeline_mode=pl.Buffered(1)),
        pl.BlockSpec((tm, 128), lambda bi, i: (i, 0)),
        pl.BlockSpec((tm, 128), lambda bi, i: (i, 0)),
        pl.BlockSpec((HEAD_DIM // 2, tm), lambda bi, i: (0, i)),
        pl.BlockSpec((HEAD_DIM // 2, tm), lambda bi, i: (0, i)),
        pl.BlockSpec((None, HIST_PAD, c_conv), lambda bi, i: (bi, 0, 0)),
        pl.BlockSpec(conv_w.shape, const2),
        pl.BlockSpec((1, c_conv), const2),
        pl.BlockSpec((1, c_conv), const2),
        pl.BlockSpec((1, c_conv), const2),
    ]
    args = [x, g_pre, w_row, w_q, cs, sn, ct, st, hist, conv_w, conv_b, ln_g, ln_b]
    aliases = {}
    if prev is not None:
        for n, arr in enumerate(prev):
            aliases[len(args)] = n
            in_specs.append(pl.BlockSpec(memory_space=pl.ANY))
            args.append(arr)
    n_alias = 0 if prev is None else len(prev)
    res = pl.pallas_call(
        functools.partial(_inproj_body, tm=tm, c_conv=c_conv, att_w=att_w, transposed=transposed,
                          n_alias=n_alias),
        grid=(b, nt),
        in_specs=in_specs,
        out_specs=stacked_specs + extra_specs,
        out_shape=stacked_shapes + extra_shapes,
        scratch_shapes=[pltpu.VMEM((tm + HIST_PAD, c_conv), F32),
                        pltpu.VMEM((7, tm + HIST_PAD - 8, c_conv), F32)],
        input_output_aliases=aliases,
        compiler_params=pltpu.CompilerParams(
            dimension_semantics=("parallel", "arbitrary"), vmem_limit_bytes=VMEM_LIMIT),
        name="inproj_conv",
    )(*args)
    return res[:4], res[4:]


def _key_to_f32(t):
    return lax.bitcast_convert_type(t ^ (lax.shift_right_arithmetic(t, 31) & 0x7FFFFFFF), F32)


def _seg_loop(segs, body, init):
    carry = init
    for base, n, rows in segs:
        if isinstance(n, int) and n == 1:
            carry = body(base, rows, carry)
        else:
            def step(kc, c, base=base, rows=rows):
                return body(pl.multiple_of(base + kc * rows, rows), rows, c)
            carry = lax.fori_loop(0, n, step, carry)
    return carry


class _ColScores:
    def __init__(self, score_ref, bias_ref, segs, tq, max_chunks=None):
        self.score_ref, self.bias_ref, self.segs, self.tq = score_ref, bias_ref, segs, tq
        self.max_chunks = max_chunks
        self.vec = (1, tq)

    def _sweep(self, body, init):
        if self.max_chunks is None:
            return _seg_loop(self.segs, body, init)
        (base, n, rows), = self.segs

        def unrolled(count):
            def run():
                carry = init
                for kc in range(count):
                    carry = body(base + kc * rows, rows, carry)
                return carry
            return run
        return lax.switch(n - 1, [unrolled(c) for c in range(1, self.max_chunks + 1)])

    def count(self, pred):
        tq = self.tq

        def body(r0, rows, acc):
            s = self.score_ref[pl.ds(r0, rows), :]
            m = jnp.where(pred(s), 1, 0).astype(I32)
            return acc + jnp.sum(m.reshape(rows // ACC_ROWS, ACC_ROWS, tq), axis=0)
        acc = self._sweep(body, jnp.zeros((ACC_ROWS, tq), I32))
        return jnp.sum(acc, axis=0, keepdims=True)

    def count_and_max_below(self, bound):
        tq = self.tq

        def body(r0, rows, carry):
            cnt, mx = carry
            s = self.score_ref[pl.ds(r0, rows), :]
            ge = s >= bound
            groups = (rows // ACC_ROWS, ACC_ROWS, tq)
            cnt = cnt + jnp.sum(jnp.where(ge, 1, 0).astype(I32).reshape(groups), axis=0)
            mx = jnp.maximum(mx, jnp.max(jnp.where(ge, -jnp.inf, s).reshape(groups), axis=0))
            return cnt, mx
        cnt, mx = self._sweep(body,
                              (jnp.zeros((ACC_ROWS, tq), I32), jnp.full((ACC_ROWS, tq), -jnp.inf, F32)))
        return jnp.sum(cnt, axis=0, keepdims=True), jnp.max(mx, axis=0, keepdims=True)

    def write_bias(self, gate, tie, need_f):
        def body(r0, rows, seen):
            s = self.score_ref[pl.ds(r0, rows), :]
            if need_f is None:
                tie_bias = 0.0
            else:
                eq = jnp.where(s == tie, 1.0, 0.0).astype(F32)
                ri = lax.broadcasted_iota(I32, (rows, rows), 0)
                ci = lax.broadcasted_iota(I32, (rows, rows), 1)
                lower = jnp.where(ci < ri, 1.0, 0.0).astype(BF16)
                before = _dot(lower, eq.astype(BF16)) + seen
                tie_bias = jnp.where(before < need_f, 0.0, -jnp.inf)
                seen = seen + jnp.sum(eq, axis=0, keepdims=True)
            self.bias_ref[pl.ds(r0, rows), :] = jnp.where(
                s > gate, 0.0, jnp.where(s == tie, tie_bias, -jnp.inf)).astype(F32)
            return seen
        _seg_loop(self.segs, body, jnp.zeros((1, self.tq), F32))


class _RowScores:
    def __init__(self, score_ref, bias_ref, widths):
        self.score_ref, self.bias_ref, self.widths = score_ref, bias_ref, widths
        self.vec = (score_ref.shape[0], 1)

    def count(self, pred):
        s = self.score_ref[...]
        return jnp.sum(jnp.where(pred(s), 1.0, 0.0).astype(F32), axis=1, keepdims=True).astype(I32)

    def count_and_max_below(self, bound):
        s = self.score_ref[...]
        ge = s >= bound
        cnt = jnp.sum(jnp.where(ge, 1.0, 0.0).astype(F32), axis=1, keepdims=True).astype(I32)
        return cnt, jnp.max(jnp.where(ge, -jnp.inf, s), axis=1, keepdims=True)

    def write_bias(self, gate, tie, need_f):
        seen = jnp.zeros(self.vec, F32)
        c0 = 0
        for width in self.widths:
            s = self.score_ref[:, c0:c0 + width]
            if need_f is None:
                tie_bias = 0.0
            else:
                eq = jnp.where(s == tie, 1.0, 0.0).astype(F32)
                ri = lax.broadcasted_iota(I32, (width, width), 0)
                ci = lax.broadcasted_iota(I32, (width, width), 1)
                upper = jnp.where(ri < ci, 1.0, 0.0).astype(BF16)
                before = _dot(eq.astype(BF16), upper) + seen
                tie_bias = jnp.where(before < need_f, 0.0, -jnp.inf)
                seen = seen + jnp.sum(eq, axis=1, keepdims=True)
            self.bias_ref[:, c0:c0 + width] = jnp.where(
                s > gate, 0.0, jnp.where(s == tie, tie_bias, -jnp.inf)).astype(F32)
            c0 += width


def _select_bias(sc, topk, n_adm):
    c0 = sc.count(lambda s: s >= 0.0)
    nonneg = c0 >= topk
    p = jnp.where(nonneg, 0, INT_MIN).astype(I32)
    c_p = jnp.where(nonneg, c0, 0).astype(I32)

    def bit_body(b, carry):
        p, c_p = carry
        t = p + lax.shift_left(jnp.int32(1), 30 - b)
        tf = _key_to_f32(t)
        c = sc.count(lambda s: s >= tf)
        take = c >= topk
        return jnp.where(take, t, p), jnp.where(take, c, c_p)

    p, c_p = lax.fori_loop(0, 31, bit_body, (p, c_p))
    p_f = _key_to_f32(p)
    c_above, v0 = sc.count_and_max_below(_key_to_f32(p + 1))
    take_all = n_adm <= topk
    settled = take_all | (v0 == p_f)
    fin0 = jnp.where(settled, 1, 0).astype(I32)

    def walk_cond(st):
        return st[0] > 0

    def walk_body(st):
        _, v, cnt_above, thr, c_thr, n_gt, fin = st
        c_ge, nxt = sc.count_and_max_below(v)
        ok = c_ge >= topk
        newly = ok & (fin == 0)
        thr = jnp.where(newly, v, thr)
        c_thr = jnp.where(newly, c_ge, c_thr)
        n_gt = jnp.where(newly, cnt_above, n_gt)
        fin = jnp.where(ok, 1, fin)
        cnt_above = jnp.where(fin > 0, cnt_above, c_ge)
        v = jnp.where(fin > 0, v, nxt)
        return jnp.sum(1 - fin), v, cnt_above, thr, c_thr, n_gt, fin

    init = (jnp.sum(1 - fin0), v0, c_above, v0, c_p, c_above, fin0)
    _, _, _, thr, c_thr, n_gt, _ = lax.while_loop(walk_cond, walk_body, init)

    need = topk - n_gt
    gate = jnp.where(take_all, -jnp.inf, thr)
    tie = jnp.where(take_all, jnp.inf, thr)
    ambiguous = jnp.logical_not(take_all) & ((c_thr - n_gt) > need)
    n_amb = jnp.sum(jnp.where(ambiguous, 1, 0).astype(I32))

    @pl.when(n_amb == 0)
    def _():
        sc.write_bias(gate, tie, None)

    @pl.when(n_amb > 0)
    def _():
        sc.write_bias(gate, tie, need.astype(F32))


def _admissible_count(qpos, s_total):
    return jnp.minimum((lax.shift_right_arithmetic(qpos, CHUNK_SHIFT) + 1) * (1 << CHUNK_SHIFT), s_total)


def _attn_prompt_body(kbf_ref, kiext_ref, vt_ref, qt_ref, qit_ref, wt_ref, out_ref,
                      score_ref, bias_ref, m_ref, l_ref, o_ref, att_ref, *, tq, topk, s_total):
    j = pl.program_id(1)
    segs = [(0, j + 1, KEY_CHUNK)]
    w = wt_ref[...]
    qpos = j * tq + lax.broadcasted_iota(I32, (1, tq), 1)
    qchunk = lax.shift_right_arithmetic(qpos, CHUNK_SHIFT)

    def score_body(r0, rows, c):
        kie = kiext_ref[pl.ds(r0, rows), :]
        s = jnp.zeros((rows, tq), F32)
        for hd in range(N_IDX_HEADS):
            lgt = _dot(kie, qit_ref[hd * 256:(hd + 1) * 256, :])
            s = s + jnp.maximum(lgt, 0.0) * w[hd:hd + 1, :]
        kpos = r0 + lax.broadcasted_iota(I32, (rows, 1), 0)
        adm = lax.shift_right_arithmetic(kpos, CHUNK_SHIFT) <= qchunk
        score_ref[pl.ds(r0, rows), :] = jnp.where(adm, s, -jnp.inf)
        return c

    _seg_loop(segs, score_body, 0)
    _select_bias(_ColScores(score_ref, bias_ref, segs, tq, max_chunks=s_total // KEY_CHUNK), topk,
                 _admissible_count(qpos, s_total))

    m_ref[...] = jnp.full(m_ref.shape, -jnp.inf, F32)
    l_ref[...] = jnp.zeros(l_ref.shape, F32)
    o_ref[...] = jnp.zeros(o_ref.shape, F32)

    def qk_body(kc, c):
        r0 = pl.multiple_of(kc * KEY_CHUNK, KEY_CHUNK)
        bias = bias_ref[pl.ds(r0, KEY_CHUNK), :]
        for hd in range(N_HEADS):
            pair = hd // 2
            kb = kbf_ref[pl.ds(r0, KEY_CHUNK), pair * 128:(pair + 1) * 128]
            a = _dot(kb, qt_ref[hd * 128:(hd + 1) * 128, :]) + bias
            att_ref[hd, pl.ds(r0, KEY_CHUNK), :] = a
            ms = slice(hd * 8, (hd + 1) * 8)
            m_ref[ms, :] = jnp.maximum(m_ref[ms, :], jnp.max(a.reshape(KEY_CHUNK // 8, 8, tq), axis=0))
        return c

    lax.fori_loop(0, j + 1, qk_body, 0)
    for hd in range(N_HEADS):
        ms = slice(hd * 8, (hd + 1) * 8)
        m_ref[ms, :] = jnp.broadcast_to(jnp.max(m_ref[ms, :], axis=0, keepdims=True), (8, tq))

    def pv_body(kc, c):
        r0 = pl.multiple_of(kc * KEY_CHUNK, KEY_CHUNK)
        for hd in range(N_HEADS):
            ms = slice(hd * 8, (hd + 1) * 8)
            hs = slice(hd * HEAD_DIM, (hd + 1) * HEAD_DIM)
            p = jnp.exp(att_ref[hd, pl.ds(r0, KEY_CHUNK), :] - m_ref[hd * 8:hd * 8 + 1, :])
            l_ref[ms, :] = l_ref[ms, :] + jnp.sum(p.reshape(KEY_CHUNK // 8, 8, tq), axis=0)
            o_ref[hs, :] = o_ref[hs, :] + _dot(vt_ref[kc, hs, :], p.astype(BF16))
        return c

    lax.fori_loop(0, j + 1, pv_body, 0)
    for hd in range(N_HEADS):
        hs = slice(hd * HEAD_DIM, (hd + 1) * HEAD_DIM)
        l = jnp.sum(l_ref[hd * 8:(hd + 1) * 8, :], axis=0, keepdims=True)
        out_ref[hs, :] = (o_ref[hs, :] / l).astype(BF16)


def _attn_prompt(kbf, kiext, vt, qt, qit, wt):
    b, t, att_w = kbf.shape
    tq = Q_TILE
    assert t % tq == 0 and tq == KEY_CHUNK and vt.shape[3] == KEY_CHUNK
    nq = t // tq
    topk = min(MAX_TOPK, t // 4)
    full = lambda shape: pl.BlockSpec((None,) + shape, lambda bi, j: (bi,) + (0,) * len(shape))
    col = lambda r: pl.BlockSpec((None, r, tq), lambda bi, j: (bi, 0, j))
    return pl.pallas_call(
        functools.partial(_attn_prompt_body, tq=tq, topk=topk, s_total=t),
        grid=(b, nq),
        in_specs=[full((t, att_w)), full((t, 256)), full((nq, att_w, KEY_CHUNK)),
                  col(N_HEADS * 128), col(N_IDX_HEADS * 256), col(N_IDX_HEADS)],
        out_specs=col(att_w),
        out_shape=jax.ShapeDtypeStruct((b, att_w, t), BF16),
        scratch_shapes=[pltpu.VMEM((t, tq), F32), pltpu.VMEM((t, tq), F32),
                        pltpu.VMEM((N_HEADS * 8, tq), F32), pltpu.VMEM((N_HEADS * 8, tq), F32),
                        pltpu.VMEM((att_w, tq), F32), pltpu.VMEM((N_HEADS, t, tq), F32)],
        compiler_params=pltpu.CompilerParams(
            dimension_semantics=("parallel", "arbitrary"), vmem_limit_bytes=VMEM_LIMIT),
        name="attn_prompt",
    )(kbf, kiext, vt, qt, qit, wt)


def _attn_sample_body(ck_ref, cv_ref, cki_ref, kn_ref, vn_ref, kin_ref, q_ref, qi_ref, w_ref, out_ref,
                      score_ref, bias_ref, att_ref, o_ref, *, s_past, ts, topk):
    s_total = s_past + ts
    rows = N_HEADS * ts
    blocks = [(c0, KEY_CHUNK) for c0 in range(0, s_past, KEY_CHUNK)] + [(s_past, NEW_KEY_PAD)]

    def key_block(ref_past, ref_new, c0, width):
        return ref_past[c0:c0 + width, :] if c0 < s_past else ref_new[...]

    def split(x):
        hi = x.astype(BF16)
        return hi, (x - hi.astype(F32)).astype(BF16)

    qi = qi_ref[...]
    qi_hi, qi_lo = split(jnp.concatenate(
        [qi[:, hd * IDX_DIM:(hd + 1) * IDX_DIM] for hd in range(N_IDX_HEADS)], axis=0))
    w = w_ref[...]
    qpos = s_past + lax.broadcasted_iota(I32, (ts, 1), 0)
    qchunk = lax.shift_right_arithmetic(qpos, CHUNK_SHIFT)
    for c0, width in blocks:
        ki_hi, ki_lo = split(key_block(cki_ref, kin_ref, c0, width))
        lgt = _dot_nt(qi_hi, ki_hi) + _dot_nt(qi_hi, ki_lo) + _dot_nt(qi_lo, ki_hi)
        s = jnp.zeros((ts, width), F32)
        for hd in range(N_IDX_HEADS):
            s = s + jnp.maximum(lgt[hd * ts:(hd + 1) * ts, :], 0.0) * w[:, hd:hd + 1]
        kpos = c0 + lax.broadcasted_iota(I32, (1, width), 1)
        adm = lax.shift_right_arithmetic(kpos, CHUNK_SHIFT) <= qchunk
        s = jnp.where(kpos < s_total, jnp.where(adm, s, -jnp.inf), -jnp.inf)
        score_ref[:, c0:c0 + width] = s

    _select_bias(_RowScores(score_ref, bias_ref, [wd for _, wd in blocks]), topk,
                 _admissible_count(qpos, s_total))

    q = q_ref[...]
    head_of_lane = lax.shift_right_arithmetic(lax.broadcasted_iota(I32, (ts, N_HEADS * HEAD_DIM), 1), HEAD_SHIFT)
    q_bd = jnp.concatenate([jnp.where(head_of_lane == hd, q, 0.0) for hd in range(N_HEADS)],
                           axis=0).astype(BF16)
    m_acc = jnp.full((rows, 128), -jnp.inf, F32)
    for c0, width in blocks:
        kb = key_block(ck_ref, kn_ref, c0, width).astype(BF16)
        a = _dot_nt(q_bd, kb)
        bias = bias_ref[:, c0:c0 + width]
        a = a + jnp.concatenate([bias] * N_HEADS, axis=0)
        att_ref[:, c0:c0 + width] = a
        for l0 in range(0, width, 128):
            m_acc = jnp.maximum(m_acc, a[:, l0:l0 + 128])
    m = jnp.max(m_acc, axis=1, keepdims=True)
    l_acc = jnp.zeros((rows, 128), F32)
    o_ref[...] = jnp.zeros(o_ref.shape, F32)
    for c0, width in blocks:
        p = jnp.exp(att_ref[:, c0:c0 + width] - m)
        for l0 in range(0, width, 128):
            l_acc = l_acc + p[:, l0:l0 + 128]
        vb = key_block(cv_ref, vn_ref, c0, width).astype(BF16)
        o_ref[...] = o_ref[...] + _dot(p.astype(BF16), vb)
    o = o_ref[...] / jnp.sum(l_acc, axis=1, keepdims=True)
    out = jnp.zeros((ts, N_HEADS * HEAD_DIM), F32)
    for hd in range(N_HEADS):
        out = out + jnp.where(head_of_lane == hd, o[hd * ts:(hd + 1) * ts, :], 0.0)
    out_ref[...] = out.astype(BF16)


def _attn_sample(cache_k, cache_v, cache_kidx, k_new, v_new, ki_new, q, qi, w, *, layer):
    _, b, s_past, att_w = cache_k.shape
    ts = q.shape[1]
    assert s_past % KEY_CHUNK == 0 and ts % 8 == 0 and ts <= NEW_KEY_PAD
    assert k_new.shape[1] == NEW_KEY_PAD
    s_pad = s_past + NEW_KEY_PAD
    topk = min(MAX_TOPK, (s_past + ts) // 4)
    full = lambda shape: pl.BlockSpec((None,) + shape, lambda bi: (bi,) + (0,) * len(shape))
    cached = lambda width: pl.BlockSpec((None, None, s_past, width), lambda bi: (layer, bi, 0, 0))
    return pl.pallas_call(
        functools.partial(_attn_sample_body, s_past=s_past, ts=ts, topk=topk),
        grid=(b,),
        in_specs=[cached(att_w), cached(att_w), cached(IDX_DIM),
                  full((NEW_KEY_PAD, att_w)), full((NEW_KEY_PAD, att_w)), full((NEW_KEY_PAD, IDX_DIM)),
                  full((ts, att_w)), full((ts, att_w)), full((ts, 128))],
        out_specs=full((ts, att_w)),
        out_shape=jax.ShapeDtypeStruct((b, ts, att_w), BF16),
        scratch_shapes=[pltpu.VMEM((ts, s_pad), F32), pltpu.VMEM((ts, s_pad), F32),
                        pltpu.VMEM((N_HEADS * ts, s_pad), F32), pltpu.VMEM((N_HEADS * ts, att_w), F32)],
        compiler_params=pltpu.CompilerParams(
            dimension_semantics=("parallel",), vmem_limit_bytes=VMEM_LIMIT),
        name="attn_sample",
    )(cache_k, cache_v, cache_kidx, k_new, v_new, ki_new, q, qi, w)


def _outproj_ffn_body(x_ref, cout_ref, att_ref, wc_ref, wa_ref, g_ref, gpre_ref, gpost_ref, wup_ref, wdown_ref,
                      o_ref, *, att_transposed, d_ff):
    m = _dot(cout_ref[...], wc_ref[...])
    if att_transposed:
        m = m + _dot_tn(att_ref[...], wa_ref[...])
    else:
        m = m + _dot(att_ref[...], wa_ref[...])
    x = x_ref[...] + _rms(m, g_ref[...])
    o_ref[...] = _half_step_ffn(x, gpre_ref[...], gpost_ref[...], wup_ref, wdown_ref, d_ff)


def _outproj_ffn(x, cout, att, w_conv, w_att, g_post, att_transposed, g_ffn_pre, g_ffn_post, w_up, w_down):
    b, t, d = x.shape
    c_conv = cout.shape[2]
    att_w = w_att.shape[0]
    d_ff = w_down.shape[0]
    tm = min(512, t)
    assert t % tm == 0
    const2 = lambda bi, i: (0, 0)
    if att_transposed:
        att_spec = pl.BlockSpec((None, att_w, tm), lambda bi, i: (bi, 0, i))
    else:
        att_spec = pl.BlockSpec((None, tm, att_w), lambda bi, i: (bi, i, 0))
    return pl.pallas_call(
        functools.partial(_outproj_ffn_body, att_transposed=att_transposed, d_ff=d_ff),
        grid=(b, t // tm),
        in_specs=[
            pl.BlockSpec((None, tm, d), lambda bi, i: (bi, i, 0)),
            pl.BlockSpec((None, tm, c_conv), lambda bi, i: (bi, i, 0)),
            att_spec,
            pl.BlockSpec(w_conv.shape, const2, pipeline_mode=pl.Buffered(1)),
            pl.BlockSpec(w_att.shape, const2, pipeline_mode=pl.Buffered(1)),
            pl.BlockSpec((1, d), const2),
            pl.BlockSpec((1, d), const2),
            pl.BlockSpec((1, d), const2),
            pl.BlockSpec((d, 2 * d_ff), const2, pipeline_mode=pl.Buffered(1)),
            pl.BlockSpec((d_ff, d), const2, pipeline_mode=pl.Buffered(1)),
        ],
        out_specs=pl.BlockSpec((None, tm, d), lambda bi, i: (bi, i, 0)),
        out_shape=jax.ShapeDtypeStruct((b, t, d), F32),
        compiler_params=pltpu.CompilerParams(
            dimension_semantics=("parallel", "parallel"), vmem_limit_bytes=VMEM_LIMIT),
        name="outproj_ffn",
    )(x, cout, att, w_conv, w_att, g_post, g_ffn_pre, g_ffn_post, w_up, w_down)


def _rope_tables(pos):
    inv = 1.0 / (ROPE_THETA ** (jnp.arange(0, HEAD_DIM, 2, dtype=F32) / HEAD_DIM))
    ang = pos.astype(F32)[:, None] * inv[None, :]
    cos, sin = jnp.cos(ang), jnp.sin(ang)
    cs = jnp.concatenate([cos, cos, cos, cos], axis=1)
    sn = jnp.concatenate([-sin, sin, -sin, sin], axis=1)
    return cs, sn, cos.T, sin.T


def _ffn_tokens(x, g_pre, g_post, w_up, w_down):
    b, t, d = x.shape
    return _ffn(x.reshape(b * t, d), g_pre, g_post, w_up, w_down).reshape(b, t, d)


def kernel(x_prompt, x_sample, cache_k, cache_v, cache_kidx, state_conv, ffn1_norm_pre, ffn1_norm_post, ffn1_w_up, ffn1_w_down, mix_norm_pre, mix_norm_post, w_in, conv_w, conv_b, conv_ln_g, conv_ln_b, w_out, ffn2_norm_pre, ffn2_norm_post, ffn2_w_up, ffn2_w_down):
    depth = w_in.shape[0]
    d = x_prompt.shape[2]
    c_conv = conv_w.shape[2]
    att_w = N_HEADS * HEAD_DIM
    assert IDX_DIM == HEAD_DIM and N_IDX_HEADS == N_HEADS
    assert w_in.shape[2] == 2 * c_conv + 3 * att_w + N_IDX_HEADS * IDX_DIM + IDX_DIM + N_IDX_HEADS
    bp, tp, _ = x_prompt.shape
    bs, ts, _ = x_sample.shape
    past_len = cache_k.shape[2]

    tabs_p = _rope_tables(jnp.arange(tp))
    tabs_s = _rope_tables(past_len + jnp.arange(ts))
    hist_p = jnp.zeros((bp, HIST_PAD, c_conv), F32)
    lead = HIST_PAD - (CONV_WIDTH - 1)
    pad_new = lambda a: jnp.pad(a, ((0, 0), (0, NEW_KEY_PAD - ts), (0, 0)))
    ck_rows = cache_k.reshape(depth, bs, past_len, att_w)
    cv_rows = cache_v.reshape(depth, bs, past_len, att_w)

    yp, ys = x_prompt, x_sample
    stack_p = stack_s = None
    for l in range(depth):
        vec = lambda a: a[l][None, :]
        o_q = 2 * c_conv
        o_k = o_q + att_w
        o_v = o_k + att_w
        o_qi = o_v + att_w
        o_ki = o_qi + N_IDX_HEADS * IDX_DIM
        o_wi = o_ki + IDX_DIM
        wl = w_in[l]
        w_row = jnp.concatenate(
            [wl[:, :o_q], wl[:, o_k:o_qi], wl[:, o_ki:o_wi], jnp.zeros((d, 128 - IDX_DIM), F32)],
            axis=1).astype(BF16)
        w_qt = jnp.concatenate(
            [wl[:, o_q:o_k], wl[:, o_qi:o_ki], wl[:, o_v:o_qi], wl[:, o_wi:],
             jnp.zeros((d, 16 - N_IDX_HEADS), F32)], axis=1).T.astype(BF16)
        w_qrow = jnp.concatenate(
            [wl[:, o_q:o_k], wl[:, o_qi:o_ki], wl[:, o_wi:], jnp.zeros((d, 128 - N_IDX_HEADS), F32)],
            axis=1).astype(BF16)
        w1u, w1d = ffn1_w_up[l].astype(BF16), ffn1_w_down[l].astype(BF16)
        w2u, w2d = ffn2_w_up[l].astype(BF16), ffn2_w_down[l].astype(BF16)
        wo_c, wo_a = w_out[l, :c_conv].astype(BF16), w_out[l, c_conv:].astype(BF16)
        hist_s = jnp.pad(state_conv[l], ((0, 0), (lead, 0), (0, 0)))

        yp = _ffn_tokens(yp, vec(ffn1_norm_pre), vec(ffn1_norm_post), w1u, w1d)
        ys = _ffn_tokens(ys, vec(ffn1_norm_pre), vec(ffn1_norm_post), w1u, w1d)

        conv_args = (conv_w[l], vec(conv_b), vec(conv_ln_g), vec(conv_ln_b))
        stack_p, (cout_p, kbf_p, kiext_p, qt_p, qit_p, vt_p, wt_p) = _inproj(
            yp, vec(mix_norm_pre), w_row, w_qt, tabs_p, hist_p, *conv_args,
            depth=depth, layer=l, prev=stack_p, transposed=True)
        stack_s, (cout_s, knew_s, vnew_s, q_s, qi_s, w_s) = _inproj(
            ys, vec(mix_norm_pre), w_row, w_qrow, tabs_s, hist_s, *conv_args,
            depth=depth, layer=l, prev=stack_s, transposed=False)

        att_p = _attn_prompt(kbf_p, kiext_p, vt_p, qt_p, qit_p, wt_p)
        att_s = _attn_sample(ck_rows, cv_rows, cache_kidx, pad_new(knew_s), pad_new(vnew_s),
                             pad_new(stack_s[2][l]), q_s, qi_s, w_s, layer=l)

        ffn2 = (vec(ffn2_norm_pre), vec(ffn2_norm_post), w2u, w2d)
        yp = _outproj_ffn(yp, cout_p, att_p, wo_c, wo_a, vec(mix_norm_post), True, *ffn2)
        flat = lambda a: a.reshape(1, bs * ts, a.shape[2])
        ys = _outproj_ffn(flat(ys), flat(cout_s), flat(att_s), wo_c, wo_a, vec(mix_norm_post), False,
                          *ffn2).reshape(bs, ts, d)

    def heads(stack, b, t):
        k, v, kidx, cstate = stack
        shape = (depth, b, t, N_HEADS, HEAD_DIM)
        return k.reshape(shape), v.reshape(shape), kidx, cstate

    return (yp, ys) + heads(stack_p, bp, tp) + heads(stack_s, bs, ts)
```

```python
import functools

import jax
import jax.numpy as jnp
from jax import lax
from jax.experimental import pallas as pl
from jax.experimental.pallas import tpu as pltpu

F32 = jnp.float32
BF16 = jnp.bfloat16
I32 = jnp.int32

CHUNK_SHIFT = 6
CONV_WIDTH = 31
N_HEADS = 8
HEAD_DIM = 64
HEAD_SHIFT = 6
N_IDX_HEADS = 8
IDX_DIM = 64
MAX_TOPK = 256
ROPE_THETA = 10000.0
EPS = 1e-6
ATT_SCALE = HEAD_DIM ** -0.5
IDX_SCALE = IDX_DIM ** -0.5
IDX_HEAD_SCALE = N_IDX_HEADS ** -0.5

KEY_CHUNK = 256
Q_TILE = 256
NEW_KEY_PAD = 128
HIST_PAD = 32
CONV_ROWS = 32
FF_CHUNK = 512
VMEM_LIMIT = 56 * 1024 * 1024

ACC_ROWS = 32
INT_MIN = -(2 ** 31)


def _rms(x, g):
    return x * lax.rsqrt(jnp.mean(x * x, axis=-1, keepdims=True) + EPS) * g


def _dot(a, b):
    return jnp.dot(a, b, preferred_element_type=F32)


def _dot_nt(a, b):
    return lax.dot_general(a, b, (((1,), (1,)), ((), ())), preferred_element_type=F32)


def _dot_tn(a, b):
    return lax.dot_general(a, b, (((0,), (0,)), ((), ())), preferred_element_type=F32)


def _half_step_ffn(x, g_pre, g_post, wup_ref, wdown_ref, d_ff):
    xn = _rms(x, g_pre).astype(BF16)
    acc = None
    for c0 in range(0, d_ff, FF_CHUNK):
        c1 = min(c0 + FF_CHUNK, d_ff)
        gate = _dot(xn, wup_ref[:, c0:c1])
        up = _dot(xn, wup_ref[:, d_ff + c0:d_ff + c1])
        act = (gate * jax.nn.sigmoid(gate) * up).astype(BF16)
        part = _dot(act, wdown_ref[c0:c1, :])
        acc = part if acc is None else acc + part
    return x + 0.5 * _rms(acc, g_post)


def _ffn_body(x_ref, gpre_ref, gpost_ref, wup_ref, wdown_ref, o_ref, *, d_ff):
    o_ref[...] = _half_step_ffn(x_ref[...], gpre_ref[...], gpost_ref[...], wup_ref, wdown_ref, d_ff)


def _ffn(x, g_pre, g_post, w_up, w_down):
    n, d = x.shape
    d_ff = w_down.shape[0]
    tm = min(512, n)
    assert n % tm == 0
    const = lambda i: (0, 0)
    return pl.pallas_call(
        functools.partial(_ffn_body, d_ff=d_ff),
        grid=(n // tm,),
        in_specs=[
            pl.BlockSpec((tm, d), lambda i: (i, 0)),
            pl.BlockSpec((1, d), const),
            pl.BlockSpec((1, d), const),
            pl.BlockSpec((d, 2 * d_ff), const, pipeline_mode=pl.Buffered(1)),
            pl.BlockSpec((d_ff, d), const, pipeline_mode=pl.Buffered(1)),
        ],
        out_specs=pl.BlockSpec((tm, d), lambda i: (i, 0)),
        out_shape=jax.ShapeDtypeStruct((n, d), F32),
        compiler_params=pltpu.CompilerParams(
            dimension_semantics=("parallel",), vmem_limit_bytes=VMEM_LIMIT),
        name="ffn",
    )(x, g_pre, g_post, w_up, w_down)


def _rope_rows(x, cs, sn):
    tm, width = x.shape
    lane = lax.broadcasted_iota(I32, (tm, width), 1)
    first_half = (lane & (HEAD_DIM - 1)) < (HEAD_DIM // 2)
    rot = jnp.where(first_half, pltpu.roll(x, width - HEAD_DIM // 2, 1), pltpu.roll(x, HEAD_DIM // 2, 1))
    reps = width // 128
    return x * jnp.concatenate([cs] * reps, axis=1) + rot * jnp.concatenate([sn] * reps, axis=1)


def _inproj_body(*refs, tm, c_conv, att_w, transposed, n_alias):
    (x_ref, g_ref, wrow_ref, wq_ref, cs_ref, sn_ref, ct_ref, st_ref, hist_ref,
     cw_ref, cb_ref, lg_ref, lb_ref) = refs[:13]
    outs = refs[13 + n_alias:]
    k_ref, v_ref, kidx_ref, cstate_ref, cout_ref = outs[:5]
    ext_ref, shift_ref = outs[-2:]
    i = pl.program_id(1)
    n_i = pl.num_programs(1)
    @pl.when(i == 0)
    def _():
        ext_ref[0:HIST_PAD, :] = hist_ref[...]

    @pl.when(i > 0)
    def _():
        ext_ref[0:HIST_PAD, :] = ext_ref[tm:tm + HIST_PAD, :]

    h = _rms(x_ref[...], g_ref[...]).astype(BF16)
    k0 = 2 * c_conv
    proj_ag = _dot(h, wrow_ref[:, 0:k0])
    proj_kv = _dot(h, wrow_ref[:, k0:k0 + 2 * att_w])
    proj_ki = _dot(h, wrow_ref[:, k0 + 2 * att_w:k0 + 2 * att_w + 128])
    if transposed:
        q_t = _dot_nt(wq_ref[0:att_w, :], h)
        qi_t = _dot_nt(wq_ref[att_w:2 * att_w, :], h)
        v_t = _dot_nt(wq_ref[2 * att_w:3 * att_w, :], h)
        w_t = _dot_nt(wq_ref[3 * att_w:3 * att_w + 16, :], h)
    else:
        proj_q = _dot(h, wq_ref[...])

    a = proj_ag[:, 0:c_conv]
    g = proj_ag[:, c_conv:2 * c_conv]
    u = a * jax.nn.sigmoid(g)
    ext_ref[HIST_PAD:HIST_PAD + tm, :] = u
    cw = cw_ref[...]
    cb = cb_ref[...]
    lg = lg_ref[...]
    lb = lb_ref[...]
    lead = HIST_PAD - (CONV_WIDTH - 1)
    rs = min(CONV_ROWS, tm)
    for sh in range(1, 8):
        shift_ref[sh - 1] = ext_ref[sh:sh + tm + HIST_PAD - 8, :]
    for r0 in range(0, tm, rs):
        acc = jnp.broadcast_to(cb, (rs, c_conv))
        for j in range(CONV_WIDTH):
            sh = (j + lead) % 8
            a8 = r0 + (j + lead) // 8 * 8
            if sh == 0:
                tap = ext_ref[a8:a8 + rs, :]
            else:
                tap = shift_ref[sh - 1, a8:a8 + rs, :]
            acc = acc + cw[j:j + 1, :] * tap
        mu = jnp.mean(acc, axis=-1, keepdims=True)
        cen = acc - mu
        var = jnp.mean(cen * cen, axis=-1, keepdims=True)
        y = cen * lax.rsqrt(var + EPS) * lg + lb
        cout_ref[r0:r0 + rs, :] = (y * jax.nn.sigmoid(y)).astype(BF16)

    cs = cs_ref[...]
    sn = sn_ref[...]
    k = _rope_rows(proj_kv[:, 0:att_w], cs, sn)
    v = proj_kv[:, att_w:2 * att_w]
    for hd in range(N_HEADS):
        k_ref[pl.ds(hd, tm, stride=N_HEADS), :] = k[:, hd * HEAD_DIM:(hd + 1) * HEAD_DIM]
        v_ref[pl.ds(hd, tm, stride=N_HEADS), :] = v[:, hd * HEAD_DIM:(hd + 1) * HEAD_DIM]

    ki = _rope_rows(proj_ki, cs, sn)
    kidx_ref[...] = ki[:, :IDX_DIM]

    if transposed:
        kbf_ref, kiext_ref, qt_ref, qit_ref, vt_ref, wto_ref = outs[5:11]
        kbf_ref[...] = k.astype(BF16)
        ki_hi = ki.astype(BF16)
        ki_lo = ki - ki_hi.astype(F32)
        kiext_ref[:, 0:128] = (ki_hi.astype(F32) + pltpu.roll(ki_lo, IDX_DIM, 1)).astype(BF16)
        kiext_ref[:, 128:256] = ki_hi

        ct = ct_ref[...]
        st = st_ref[...]
        half = HEAD_DIM // 2
        zeros_half = jnp.zeros((HEAD_DIM, tm), BF16)
        for hd in range(N_HEADS):
            x1 = q_t[hd * HEAD_DIM:hd * HEAD_DIM + half, :]
            x2 = q_t[hd * HEAD_DIM + half:(hd + 1) * HEAD_DIM, :]
            base = hd * 128 + (hd % 2) * HEAD_DIM
            other = hd * 128 + ((hd + 1) % 2) * HEAD_DIM
            qt_ref[base:base + half, :] = ((x1 * ct - x2 * st) * ATT_SCALE).astype(BF16)
            qt_ref[base + half:base + HEAD_DIM, :] = ((x2 * ct + x1 * st) * ATT_SCALE).astype(BF16)
            qt_ref[other:other + HEAD_DIM, :] = zeros_half
        for hd in range(N_IDX_HEADS):
            r = hd * IDX_DIM
            x1 = qi_t[r:r + half, :]
            x2 = qi_t[r + half:r + IDX_DIM, :]
            y1 = (x1 * ct - x2 * st) * IDX_SCALE
            y2 = (x2 * ct + x1 * st) * IDX_SCALE
            y1h = y1.astype(BF16)
            y2h = y2.astype(BF16)
            y1l = (y1 - y1h.astype(F32)).astype(BF16)
            y2l = (y2 - y2h.astype(F32)).astype(BF16)
            b = hd * 256
            qit_ref[b:b + half, :] = y1h
            qit_ref[b + half:b + 64, :] = y2h
            qit_ref[b + 64:b + 64 + half, :] = y1h
            qit_ref[b + 64 + half:b + 128, :] = y2h
            qit_ref[b + 128:b + 128 + half, :] = y1l
            qit_ref[b + 128 + half:b + 192, :] = y2l
            qit_ref[b + 192:b + 256, :] = zeros_half
        vt_ref[...] = v_t.astype(BF16)
        wto_ref[...] = w_t[0:N_IDX_HEADS, :] * IDX_HEAD_SCALE
    else:
        knew_ref, vnew_ref, q_ref, qi_ref, w_ref = outs[5:10]
        knew_ref[...] = k
        vnew_ref[...] = v
        q_ref[...] = _rope_rows(proj_q[:, 0:att_w], cs, sn) * ATT_SCALE
        qi_ref[...] = _rope_rows(proj_q[:, att_w:2 * att_w], cs, sn) * IDX_SCALE
        w_ref[...] = proj_q[:, 2 * att_w:2 * att_w + 128] * IDX_HEAD_SCALE

    @pl.when(i == n_i - 1)
    def _():
        cstate_ref[...] = ext_ref[tm + lead:tm + HIST_PAD, :]


def _inproj(x, g_pre, w_row, w_q, tabs, hist, conv_w, conv_b, ln_g, ln_b, *, depth, layer, prev, transposed):
    b, t, d = x.shape
    c_conv = conv_w.shape[1]
    att_w = N_HEADS * HEAD_DIM
    tm = min(Q_TILE, t)
    assert t % tm == 0
    nt = t // tm
    cs, sn, ct, st = tabs
    const2 = lambda bi, i: (0, 0)
    row = lambda w: pl.BlockSpec((None, tm, w), lambda bi, i: (bi, i, 0))
    col = lambda r: pl.BlockSpec((None, r, tm), lambda bi, i: (bi, 0, i))
    stacked_shapes = (
        jax.ShapeDtypeStruct((depth, b, t * N_HEADS, HEAD_DIM), F32),
        jax.ShapeDtypeStruct((depth, b, t * N_HEADS, HEAD_DIM), F32),
        jax.ShapeDtypeStruct((depth, b, t, IDX_DIM), F32),
        jax.ShapeDtypeStruct((depth, b, CONV_WIDTH - 1, c_conv), F32),
    )
    stacked_specs = (
        pl.BlockSpec((None, None, tm * N_HEADS, HEAD_DIM), lambda bi, i: (layer, bi, i, 0)),
        pl.BlockSpec((None, None, tm * N_HEADS, HEAD_DIM), lambda bi, i: (layer, bi, i, 0)),
        pl.BlockSpec((None, None, tm, IDX_DIM), lambda bi, i: (layer, bi, i, 0)),
        pl.BlockSpec((None, None, CONV_WIDTH - 1, c_conv), lambda bi, i: (layer, bi, 0, 0)),
    )
    if transposed:
        extra_shapes = (
            jax.ShapeDtypeStruct((b, t, c_conv), BF16),
            jax.ShapeDtypeStruct((b, t, att_w), BF16),
            jax.ShapeDtypeStruct((b, t, 256), BF16),
            jax.ShapeDtypeStruct((b, N_HEADS * 128, t), BF16),
            jax.ShapeDtypeStruct((b, N_IDX_HEADS * 256, t), BF16),
            jax.ShapeDtypeStruct((b, nt, att_w, tm), BF16),
            jax.ShapeDtypeStruct((b, N_IDX_HEADS, t), F32),
        )
        extra_specs = (
            row(c_conv), row(att_w), row(256), col(N_HEADS * 128), col(N_IDX_HEADS * 256),
            pl.BlockSpec((None, None, att_w, tm), lambda bi, i: (bi, i, 0, 0)),
            col(N_IDX_HEADS),
        )
    else:
        extra_shapes = (
            jax.ShapeDtypeStruct((b, t, c_conv), BF16),
            jax.ShapeDtypeStruct((b, t, att_w), F32),
            jax.ShapeDtypeStruct((b, t, att_w), F32),
            jax.ShapeDtypeStruct((b, t, att_w), F32),
            jax.ShapeDtypeStruct((b, t, att_w), F32),
            jax.ShapeDtypeStruct((b, t, 128), F32),
        )
        extra_specs = (row(c_conv), row(att_w), row(att_w), row(att_w), row(att_w), row(128))
    in_specs = [
        pl.BlockSpec((None, tm, d), lambda bi, i: (bi, i, 0)),
        pl.BlockSpec((1, d), const2),
        pl.BlockSpec(w_row.shape, const2, pipeline_mode=pl.Buffered(1)),
        pl.BlockSpec(w_q.shape, const2, pipeline_mode=pl.Buffered(1)),
        pl.BlockSpec((tm, 128), lambda bi, i: (i, 0)),
        pl.BlockSpec((tm, 128), lambda bi, i: (i, 0)),
        pl.BlockSpec((HEAD_DIM // 2, tm), lambda bi, i: (0, i)),
        pl.BlockSpec((HEAD_DIM // 2, tm), lambda bi, i: (0, i)),
        pl.BlockSpec((None, HIST_PAD, c_conv), lambda bi, i: (bi, 0, 0)),
        pl.BlockSpec(conv_w.shape, const2),
        pl.BlockSpec((1, c_conv), const2),
        pl.BlockSpec((1, c_conv), const2),
        pl.BlockSpec((1, c_conv), const2),
    ]
    args = [x, g_pre, w_row, w_q, cs, sn, ct, st, hist, conv_w, conv_b, ln_g, ln_b]
    aliases = {}
    if prev is not None:
        for n, arr in enumerate(prev):
            aliases[len(args)] = n
            in_specs.append(pl.BlockSpec(memory_space=pl.ANY))
            args.append(arr)
    n_alias = 0 if prev is None else len(prev)
    res = pl.pallas_call(
        functools.partial(_inproj_body, tm=tm, c_conv=c_conv, att_w=att_w, transposed=transposed,
                          n_alias=n_alias),
        grid=(b, nt),
        in_specs=in_specs,
        out_specs=stacked_specs + extra_specs,
        out_shape=stacked_shapes + extra_shapes,
        scratch_shapes=[pltpu.VMEM((tm + HIST_PAD, c_conv), F32),
                        pltpu.VMEM((7, tm + HIST_PAD - 8, c_conv), F32)],
        input_output_aliases=aliases,
        compiler_params=pltpu.CompilerParams(
            dimension_semantics=("parallel", "arbitrary"), vmem_limit_bytes=VMEM_LIMIT),
        name="inproj_conv",
    )(*args)
    return res[:4], res[4:]


def _key_to_f32(t):
    return lax.bitcast_convert_type(t ^ (lax.shift_right_arithmetic(t, 31) & 0x7FFFFFFF), F32)


def _seg_loop(segs, body, init):
    carry = init
    for base, n, rows in segs:
        if isinstance(n, int) and n == 1:
            carry = body(base, rows, carry)
        else:
            def step(kc, c, base=base, rows=rows):
                return body(pl.multiple_of(base + kc * rows, rows), rows, c)
            carry = lax.fori_loop(0, n, step, carry)
    return carry


class _ColScores:
    def __init__(self, score_ref, bias_ref, segs, tq, max_chunks=None):
        self.score_ref, self.bias_ref, self.segs, self.tq = score_ref, bias_ref, segs, tq
        self.max_chunks = max_chunks
        self.vec = (1, tq)

    def _sweep(self, body, init):
        if self.max_chunks is None:
            return _seg_loop(self.segs, body, init)
        (base, n, rows), = self.segs

        def unrolled(count):
            def run():
                carry = init
                for kc in range(count):
                    carry = body(base + kc * rows, rows, carry)
                return carry
            return run
        return lax.switch(n - 1, [unrolled(c) for c in range(1, self.max_chunks + 1)])

    def count(self, pred):
        tq = self.tq

        def body(r0, rows, acc):
            s = self.score_ref[pl.ds(r0, rows), :]
            m = jnp.where(pred(s), 1, 0).astype(I32)
            return acc + jnp.sum(m.reshape(rows // ACC_ROWS, ACC_ROWS, tq), axis=0)
        acc = self._sweep(body, jnp.zeros((ACC_ROWS, tq), I32))
        return jnp.sum(acc, axis=0, keepdims=True)

    def count_and_max_below(self, bound):
        tq = self.tq

        def body(r0, rows, carry):
            cnt, mx = carry
            s = self.score_ref[pl.ds(r0, rows), :]
            ge = s >= bound
            groups = (rows // ACC_ROWS, ACC_ROWS, tq)
            cnt = cnt + jnp.sum(jnp.where(ge, 1, 0).astype(I32).reshape(groups), axis=0)
            mx = jnp.maximum(mx, jnp.max(jnp.where(ge, -jnp.inf, s).reshape(groups), axis=0))
            return cnt, mx
        cnt, mx = self._sweep(body,
                              (jnp.zeros((ACC_ROWS, tq), I32), jnp.full((ACC_ROWS, tq), -jnp.inf, F32)))
        return jnp.sum(cnt, axis=0, keepdims=True), jnp.max(mx, axis=0, keepdims=True)

    def write_bias(self, gate, tie, need_f):
        def body(r0, rows, seen):
            s = self.score_ref[pl.ds(r0, rows), :]
            if need_f is None:
                tie_bias = 0.0
            else:
                eq = jnp.where(s == tie, 1.0, 0.0).astype(F32)
                ri = lax.broadcasted_iota(I32, (rows, rows), 0)
                ci = lax.broadcasted_iota(I32, (rows, rows), 1)
                lower = jnp.where(ci < ri, 1.0, 0.0).astype(BF16)
                before = _dot(lower, eq.astype(BF16)) + seen
                tie_bias = jnp.where(before < need_f, 0.0, -jnp.inf)
                seen = seen + jnp.sum(eq, axis=0, keepdims=True)
            self.bias_ref[pl.ds(r0, rows), :] = jnp.where(
                s > gate, 0.0, jnp.where(s == tie, tie_bias, -jnp.inf)).astype(F32)
            return seen
        _seg_loop(self.segs, body, jnp.zeros((1, self.tq), F32))


class _RowScores:
    def __init__(self, score_ref, bias_ref, widths):
        self.score_ref, self.bias_ref, self.widths = score_ref, bias_ref, widths
        self.vec = (score_ref.shape[0], 1)

    def count(self, pred):
        s = self.score_ref[...]
        return jnp.sum(jnp.where(pred(s), 1.0, 0.0).astype(F32), axis=1, keepdims=True).astype(I32)

    def count_and_max_below(self, bound):
        s = self.score_ref[...]
        ge = s >= bound
        cnt = jnp.sum(jnp.where(ge, 1.0, 0.0).astype(F32), axis=1, keepdims=True).astype(I32)
        return cnt, jnp.max(jnp.where(ge, -jnp.inf, s), axis=1, keepdims=True)

    def write_bias(self, gate, tie, need_f):
        seen = jnp.zeros(self.vec, F32)
        c0 = 0
        for width in self.widths:
            s = self.score_ref[:, c0:c0 + width]
            if need_f is None:
                tie_bias = 0.0
            else:
                eq = jnp.where(s == tie, 1.0, 0.0).astype(F32)
                ri = lax.broadcasted_iota(I32, (width, width), 0)
                ci = lax.broadcasted_iota(I32, (width, width), 1)
                upper = jnp.where(ri < ci, 1.0, 0.0).astype(BF16)
                before = _dot(eq.astype(BF16), upper) + seen
                tie_bias = jnp.where(before < need_f, 0.0, -jnp.inf)
                seen = seen + jnp.sum(eq, axis=1, keepdims=True)
            self.bias_ref[:, c0:c0 + width] = jnp.where(
                s > gate, 0.0, jnp.where(s == tie, tie_bias, -jnp.inf)).astype(F32)
            c0 += width


def _select_bias(sc, topk, n_adm):
    c0 = sc.count(lambda s: s >= 0.0)
    nonneg = c0 >= topk
    p = jnp.where(nonneg, 0, INT_MIN).astype(I32)
    c_p = jnp.where(nonneg, c0, 0).astype(I32)

    def bit_body(b, carry):
        p, c_p = carry
        t = p + lax.shift_left(jnp.int32(1), 30 - b)
        tf = _key_to_f32(t)
        c = sc.count(lambda s: s >= tf)
        take = c >= topk
        return jnp.where(take, t, p), jnp.where(take, c, c_p)

    p, c_p = lax.fori_loop(0, 31, bit_body, (p, c_p))
    p_f = _key_to_f32(p)
    c_above, v0 = sc.count_and_max_below(_key_to_f32(p + 1))
    take_all = n_adm <= topk
    settled = take_all | (v0 == p_f)
    fin0 = jnp.where(settled, 1, 0).astype(I32)

    def walk_cond(st):
        return st[0] > 0

    def walk_body(st):
        _, v, cnt_above, thr, c_thr, n_gt, fin = st
        c_ge, nxt = sc.count_and_max_below(v)
        ok = c_ge >= topk
        newly = ok & (fin == 0)
        thr = jnp.where(newly, v, thr)
        c_thr = jnp.where(newly, c_ge, c_thr)
        n_gt = jnp.where(newly, cnt_above, n_gt)
        fin = jnp.where(ok, 1, fin)
        cnt_above = jnp.where(fin > 0, cnt_above, c_ge)
        v = jnp.where(fin > 0, v, nxt)
        return jnp.sum(1 - fin), v, cnt_above, thr, c_thr, n_gt, fin

    init = (jnp.sum(1 - fin0), v0, c_above, v0, c_p, c_above, fin0)
    _, _, _, thr, c_thr, n_gt, _ = lax.while_loop(walk_cond, walk_body, init)

    need = topk - n_gt
    gate = jnp.where(take_all, -jnp.inf, thr)
    tie = jnp.where(take_all, jnp.inf, thr)
    ambiguous = jnp.logical_not(take_all) & ((c_thr - n_gt) > need)
    n_amb = jnp.sum(jnp.where(ambiguous, 1, 0).astype(I32))

    @pl.when(n_amb == 0)
    def _():
        sc.write_bias(gate, tie, None)

    @pl.when(n_amb > 0)
    def _():
        sc.write_bias(gate, tie, need.astype(F32))


def _admissible_count(qpos, s_total):
    return jnp.minimum((lax.shift_right_arithmetic(qpos, CHUNK_SHIFT) + 1) * (1 << CHUNK_SHIFT), s_total)


def _attn_prompt_body(kbf_ref, kiext_ref, vt_ref, qt_ref, qit_ref, wt_ref, out_ref,
                      score_ref, bias_ref, m_ref, l_ref, o_ref, att_ref, *, tq, topk, s_total):
    j = pl.program_id(1)
    segs = [(0, j + 1, KEY_CHUNK)]
    w = wt_ref[...]
    qpos = j * tq + lax.broadcasted_iota(I32, (1, tq), 1)
    qchunk = lax.shift_right_arithmetic(qpos, CHUNK_SHIFT)

    n_tiles = s_total // KEY_CHUNK

    def score_rows(n_rows):
        def run():
            kie = kiext_ref[0:n_rows, :]
            s = jnp.zeros((n_rows, tq), F32)
            for hd in range(N_IDX_HEADS):
                lgt = _dot(kie, qit_ref[hd * 256:(hd + 1) * 256, :])
                s = s + jnp.maximum(lgt, 0.0) * w[hd:hd + 1, :]
            kpos = lax.broadcasted_iota(I32, (n_rows, 1), 0)
            adm = lax.shift_right_arithmetic(kpos, CHUNK_SHIFT) <= qchunk
            score_ref[0:n_rows, :] = jnp.where(adm, s, -jnp.inf)
            return jnp.int32(0)
        return run

    lax.switch(j, [score_rows((c + 1) * KEY_CHUNK) for c in range(n_tiles)])
    _select_bias(_ColScores(score_ref, bias_ref, segs, tq, max_chunks=s_total // KEY_CHUNK), topk,
                 _admissible_count(qpos, s_total))

    m_ref[...] = jnp.full(m_ref.shape, -jnp.inf, F32)
    l_ref[...] = jnp.zeros(l_ref.shape, F32)
    o_ref[...] = jnp.zeros(o_ref.shape, F32)

    def qk_rows(n_rows):
        def run():
            bias = bias_ref[0:n_rows, :]
            for hd in range(N_HEADS):
                pair = hd // 2
                kb = kbf_ref[0:n_rows, pair * 128:(pair + 1) * 128]
                a = _dot(kb, qt_ref[hd * 128:(hd + 1) * 128, :]) + bias
                att_ref[hd, 0:n_rows, :] = a
                m_part = jnp.max(a.reshape(n_rows // ACC_ROWS, ACC_ROWS, tq), axis=0)
                m_ref[hd * 8:(hd + 1) * 8, :] = jnp.max(m_part.reshape(ACC_ROWS // 8, 8, tq), axis=0)
            return jnp.int32(0)
        return run

    lax.switch(j, [qk_rows((c + 1) * KEY_CHUNK) for c in range(n_tiles)])
    for hd in range(N_HEADS):
        ms = slice(hd * 8, (hd + 1) * 8)
        m_ref[ms, :] = jnp.broadcast_to(jnp.max(m_ref[ms, :], axis=0, keepdims=True), (8, tq))

    def pv_body(kc, c):
        r0 = pl.multiple_of(kc * KEY_CHUNK, KEY_CHUNK)
        for hd in range(N_HEADS):
            ms = slice(hd * 8, (hd + 1) * 8)
            hs = slice(hd * HEAD_DIM, (hd + 1) * HEAD_DIM)
            p = jnp.exp(att_ref[hd, pl.ds(r0, KEY_CHUNK), :] - m_ref[hd * 8:hd * 8 + 1, :])
            l_ref[ms, :] = l_ref[ms, :] + jnp.sum(p.reshape(KEY_CHUNK // 8, 8, tq), axis=0)
            o_ref[hs, :] = o_ref[hs, :] + _dot(vt_ref[kc, hs, :], p.astype(BF16))
        return c

    lax.fori_loop(0, j + 1, pv_body, 0)
    for hd in range(N_HEADS):
        hs = slice(hd * HEAD_DIM, (hd + 1) * HEAD_DIM)
        l = jnp.sum(l_ref[hd * 8:(hd + 1) * 8, :], axis=0, keepdims=True)
        out_ref[hs, :] = (o_ref[hs, :] / l).astype(BF16)


def _attn_prompt(kbf, kiext, vt, qt, qit, wt):
    b, t, att_w = kbf.shape
    tq = Q_TILE
    assert t % tq == 0 and tq == KEY_CHUNK and vt.shape[3] == KEY_CHUNK
    nq = t // tq
    topk = min(MAX_TOPK, t // 4)
    full = lambda shape: pl.BlockSpec((None,) + shape, lambda bi, j: (bi,) + (0,) * len(shape))
    col = lambda r: pl.BlockSpec((None, r, tq), lambda bi, j: (bi, 0, j))
    return pl.pallas_call(
        functools.partial(_attn_prompt_body, tq=tq, topk=topk, s_total=t),
        grid=(b, nq),
        in_specs=[full((t, att_w)), full((t, 256)), full((nq, att_w, KEY_CHUNK)),
                  col(N_HEADS * 128), col(N_IDX_HEADS * 256), col(N_IDX_HEADS)],
        out_specs=col(att_w),
        out_shape=jax.ShapeDtypeStruct((b, att_w, t), BF16),
        scratch_shapes=[pltpu.VMEM((t, tq), F32), pltpu.VMEM((t, tq), F32),
                        pltpu.VMEM((N_HEADS * 8, tq), F32), pltpu.VMEM((N_HEADS * 8, tq), F32),
                        pltpu.VMEM((att_w, tq), F32), pltpu.VMEM((N_HEADS, t, tq), F32)],
        compiler_params=pltpu.CompilerParams(
            dimension_semantics=("parallel", "arbitrary"), vmem_limit_bytes=VMEM_LIMIT),
        name="attn_prompt",
    )(kbf, kiext, vt, qt, qit, wt)


def _attn_sample_body(ck_ref, cv_ref, cki_ref, kn_ref, vn_ref, kin_ref, q_ref, qi_ref, w_ref, out_ref,
                      score_ref, bias_ref, att_ref, o_ref, *, s_past, ts, topk):
    s_total = s_past + ts
    rows = N_HEADS * ts
    blocks = [(c0, KEY_CHUNK) for c0 in range(0, s_past, KEY_CHUNK)] + [(s_past, NEW_KEY_PAD)]

    def key_block(ref_past, ref_new, c0, width):
        return ref_past[c0:c0 + width, :] if c0 < s_past else ref_new[...]

    def split(x):
        hi = x.astype(BF16)
        return hi, (x - hi.astype(F32)).astype(BF16)

    qi = qi_ref[...]
    qi_hi, qi_lo = split(jnp.concatenate(
        [qi[:, hd * IDX_DIM:(hd + 1) * IDX_DIM] for hd in range(N_IDX_HEADS)], axis=0))
    w = w_ref[...]
    qpos = s_past + lax.broadcasted_iota(I32, (ts, 1), 0)
    qchunk = lax.shift_right_arithmetic(qpos, CHUNK_SHIFT)
    for c0, width in blocks:
        ki_hi, ki_lo = split(key_block(cki_ref, kin_ref, c0, width))
        lgt = _dot_nt(qi_hi, ki_hi) + _dot_nt(qi_hi, ki_lo) + _dot_nt(qi_lo, ki_hi)
        s = jnp.zeros((ts, width), F32)
        for hd in range(N_IDX_HEADS):
            s = s + jnp.maximum(lgt[hd * ts:(hd + 1) * ts, :], 0.0) * w[:, hd:hd + 1]
        kpos = c0 + lax.broadcasted_iota(I32, (1, width), 1)
        adm = lax.shift_right_arithmetic(kpos, CHUNK_SHIFT) <= qchunk
        s = jnp.where(kpos < s_total, jnp.where(adm, s, -jnp.inf), -jnp.inf)
        score_ref[:, c0:c0 + width] = s

    _select_bias(_RowScores(score_ref, bias_ref, [wd for _, wd in blocks]), topk,
                 _admissible_count(qpos, s_total))

    q = q_ref[...]
    head_of_lane = lax.shift_right_arithmetic(lax.broadcasted_iota(I32, (ts, N_HEADS * HEAD_DIM), 1), HEAD_SHIFT)
    q_bd = jnp.concatenate([jnp.where(head_of_lane == hd, q, 0.0) for hd in range(N_HEADS)],
                           axis=0).astype(BF16)
    m_acc = jnp.full((rows, 128), -jnp.inf, F32)
    for c0, width in blocks:
        kb = key_block(ck_ref, kn_ref, c0, width).astype(BF16)
        a = _dot_nt(q_bd, kb)
        bias = bias_ref[:, c0:c0 + width]
        a = a + jnp.concatenate([bias] * N_HEADS, axis=0)
        att_ref[:, c0:c0 + width] = a
        for l0 in range(0, width, 128):
            m_acc = jnp.maximum(m_acc, a[:, l0:l0 + 128])
    m = jnp.max(m_acc, axis=1, keepdims=True)
    l_acc = jnp.zeros((rows, 128), F32)
    o_ref[...] = jnp.zeros(o_ref.shape, F32)
    for c0, width in blocks:
        p = jnp.exp(att_ref[:, c0:c0 + width] - m)
        for l0 in range(0, width, 128):
            l_acc = l_acc + p[:, l0:l0 + 128]
        vb = key_block(cv_ref, vn_ref, c0, width).astype(BF16)
        o_ref[...] = o_ref[...] + _dot(p.astype(BF16), vb)
    o = o_ref[...] / jnp.sum(l_acc, axis=1, keepdims=True)
    out = jnp.zeros((ts, N_HEADS * HEAD_DIM), F32)
    for hd in range(N_HEADS):
        out = out + jnp.where(head_of_lane == hd, o[hd * ts:(hd + 1) * ts, :], 0.0)
    out_ref[...] = out.astype(BF16)


def _attn_sample(cache_k, cache_v, cache_kidx, k_new, v_new, ki_new, q, qi, w, *, layer):
    _, b, s_past, att_w = cache_k.shape
    ts = q.shape[1]
    assert s_past % KEY_CHUNK == 0 and ts % 8 == 0 and ts <= NEW_KEY_PAD
    assert k_new.shape[1] == NEW_KEY_PAD
    s_pad = s_past + NEW_KEY_PAD
    topk = min(MAX_TOPK, (s_past + ts) // 4)
    full = lambda shape: pl.BlockSpec((None,) + shape, lambda bi: (bi,) + (0,) * len(shape))
    cached = lambda width: pl.BlockSpec((None, None, s_past, width), lambda bi: (layer, bi, 0, 0))
    return pl.pallas_call(
        functools.partial(_attn_sample_body, s_past=s_past, ts=ts, topk=topk),
        grid=(b,),
        in_specs=[cached(att_w), cached(att_w), cached(IDX_DIM),
                  full((NEW_KEY_PAD, att_w)), full((NEW_KEY_PAD, att_w)), full((NEW_KEY_PAD, IDX_DIM)),
                  full((ts, att_w)), full((ts, att_w)), full((ts, 128))],
        out_specs=full((ts, att_w)),
        out_shape=jax.ShapeDtypeStruct((b, ts, att_w), BF16),
        scratch_shapes=[pltpu.VMEM((ts, s_pad), F32), pltpu.VMEM((ts, s_pad), F32),
                        pltpu.VMEM((N_HEADS * ts, s_pad), F32), pltpu.VMEM((N_HEADS * ts, att_w), F32)],
        compiler_params=pltpu.CompilerParams(
            dimension_semantics=("parallel",), vmem_limit_bytes=VMEM_LIMIT),
        name="attn_sample",
    )(cache_k, cache_v, cache_kidx, k_new, v_new, ki_new, q, qi, w)


def _outproj_ffn_body(x_ref, cout_ref, att_ref, wc_ref, wa_ref, g_ref, gpre_ref, gpost_ref, wup_ref, wdown_ref,
                      o_ref, *, att_transposed, d_ff):
    m = _dot(cout_ref[...], wc_ref[...])
    if att_transposed:
        m = m + _dot_tn(att_ref[...], wa_ref[...])
    else:
        m = m + _dot(att_ref[...], wa_ref[...])
    x = x_ref[...] + _rms(m, g_ref[...])
    o_ref[...] = _half_step_ffn(x, gpre_ref[...], gpost_ref[...], wup_ref, wdown_ref, d_ff)


def _outproj_ffn(x, cout, att, w_conv, w_att, g_post, att_transposed, g_ffn_pre, g_ffn_post, w_up, w_down):
    b, t, d = x.shape
    c_conv = cout.shape[2]
    att_w = w_att.shape[0]
    d_ff = w_down.shape[0]
    tm = min(512, t)
    assert t % tm == 0
    const2 = lambda bi, i: (0, 0)
    if att_transposed:
        att_spec = pl.BlockSpec((None, att_w, tm), lambda bi, i: (bi, 0, i))
    else:
        att_spec = pl.BlockSpec((None, tm, att_w), lambda bi, i: (bi, i, 0))
    return pl.pallas_call(
        functools.partial(_outproj_ffn_body, att_transposed=att_transposed, d_ff=d_ff),
        grid=(b, t // tm),
        in_specs=[
            pl.BlockSpec((None, tm, d), lambda bi, i: (bi, i, 0)),
            pl.BlockSpec((None, tm, c_conv), lambda bi, i: (bi, i, 0)),
            att_spec,
            pl.BlockSpec(w_conv.shape, const2, pipeline_mode=pl.Buffered(1)),
            pl.BlockSpec(w_att.shape, const2, pipeline_mode=pl.Buffered(1)),
            pl.BlockSpec((1, d), const2),
            pl.BlockSpec((1, d), const2),
            pl.BlockSpec((1, d), const2),
            pl.BlockSpec((d, 2 * d_ff), const2, pipeline_mode=pl.Buffered(1)),
            pl.BlockSpec((d_ff, d), const2, pipeline_mode=pl.Buffered(1)),
        ],
        out_specs=pl.BlockSpec((None, tm, d), lambda bi, i: (bi, i, 0)),
        out_shape=jax.ShapeDtypeStruct((b, t, d), F32),
        compiler_params=pltpu.CompilerParams(
            dimension_semantics=("parallel", "parallel"), vmem_limit_bytes=VMEM_LIMIT),
        name="outproj_ffn",
    )(x, cout, att, w_conv, w_att, g_post, g_ffn_pre, g_ffn_post, w_up, w_down)


def _rope_tables(pos):
    inv = 1.0 / (ROPE_THETA ** (jnp.arange(0, HEAD_DIM, 2, dtype=F32) / HEAD_DIM))
    ang = pos.astype(F32)[:, None] * inv[None, :]
    cos, sin = jnp.cos(ang), jnp.sin(ang)
    cs = jnp.concatenate([cos, cos, cos, cos], axis=1)
    sn = jnp.concatenate([-sin, sin, -sin, sin], axis=1)
    return cs, sn, cos.T, sin.T


def _ffn_tokens(x, g_pre, g_post, w_up, w_down):
    b, t, d = x.shape
    return _ffn(x.reshape(b * t, d), g_pre, g_post, w_up, w_down).reshape(b, t, d)


def kernel(x_prompt, x_sample, cache_k, cache_v, cache_kidx, state_conv, ffn1_norm_pre, ffn1_norm_post, ffn1_w_up, ffn1_w_down, mix_norm_pre, mix_norm_post, w_in, conv_w, conv_b, conv_ln_g, conv_ln_b, w_out, ffn2_norm_pre, ffn2_norm_post, ffn2_w_up, ffn2_w_down):
    depth = w_in.shape[0]
    d = x_prompt.shape[2]
    c_conv = conv_w.shape[2]
    att_w = N_HEADS * HEAD_DIM
    assert IDX_DIM == HEAD_DIM and N_IDX_HEADS == N_HEADS
    assert w_in.shape[2] == 2 * c_conv + 3 * att_w + N_IDX_HEADS * IDX_DIM + IDX_DIM + N_IDX_HEADS
    bp, tp, _ = x_prompt.shape
    bs, ts, _ = x_sample.shape
    past_len = cache_k.shape[2]

    tabs_p = _rope_tables(jnp.arange(tp))
    tabs_s = _rope_tables(past_len + jnp.arange(ts))
    hist_p = jnp.zeros((bp, HIST_PAD, c_conv), F32)
    lead = HIST_PAD - (CONV_WIDTH - 1)
    pad_new = lambda a: jnp.pad(a, ((0, 0), (0, NEW_KEY_PAD - ts), (0, 0)))
    ck_rows = cache_k.reshape(depth, bs, past_len, att_w)
    cv_rows = cache_v.reshape(depth, bs, past_len, att_w)

    yp, ys = x_prompt, x_sample
    stack_p = stack_s = None
    for l in range(depth):
        vec = lambda a: a[l][None, :]
        o_q = 2 * c_conv
        o_k = o_q + att_w
        o_v = o_k + att_w
        o_qi = o_v + att_w
        o_ki = o_qi + N_IDX_HEADS * IDX_DIM
        o_wi = o_ki + IDX_DIM
        wl = w_in[l]
        w_row = jnp.concatenate(
            [wl[:, :o_q], wl[:, o_k:o_qi], wl[:, o_ki:o_wi], jnp.zeros((d, 128 - IDX_DIM), F32)],
            axis=1).astype(BF16)
        w_qt = jnp.concatenate(
            [wl[:, o_q:o_k], wl[:, o_qi:o_ki], wl[:, o_v:o_qi], wl[:, o_wi:],
             jnp.zeros((d, 16 - N_IDX_HEADS), F32)], axis=1).T.astype(BF16)
        w_qrow = jnp.concatenate(
            [wl[:, o_q:o_k], wl[:, o_qi:o_ki], wl[:, o_wi:], jnp.zeros((d, 128 - N_IDX_HEADS), F32)],
            axis=1).astype(BF16)
        w1u, w1d = ffn1_w_up[l].astype(BF16), ffn1_w_down[l].astype(BF16)
        w2u, w2d = ffn2_w_up[l].astype(BF16), ffn2_w_down[l].astype(BF16)
        wo_c, wo_a = w_out[l, :c_conv].astype(BF16), w_out[l, c_conv:].astype(BF16)
        hist_s = jnp.pad(state_conv[l], ((0, 0), (lead, 0), (0, 0)))

        yp = _ffn_tokens(yp, vec(ffn1_norm_pre), vec(ffn1_norm_post), w1u, w1d)
        ys = _ffn_tokens(ys, vec(ffn1_norm_pre), vec(ffn1_norm_post), w1u, w1d)

        conv_args = (conv_w[l], vec(conv_b), vec(conv_ln_g), vec(conv_ln_b))
        stack_p, (cout_p, kbf_p, kiext_p, qt_p, qit_p, vt_p, wt_p) = _inproj(
            yp, vec(mix_norm_pre), w_row, w_qt, tabs_p, hist_p, *conv_args,
            depth=depth, layer=l, prev=stack_p, transposed=True)
        stack_s, (cout_s, knew_s, vnew_s, q_s, qi_s, w_s) = _inproj(
            ys, vec(mix_norm_pre), w_row, w_qrow, tabs_s, hist_s, *conv_args,
            depth=depth, layer=l, prev=stack_s, transposed=False)

        att_p = _attn_prompt(kbf_p, kiext_p, vt_p, qt_p, qit_p, wt_p)
        att_s = _attn_sample(ck_rows, cv_rows, cache_kidx, pad_new(knew_s), pad_new(vnew_s),
                             pad_new(stack_s[2][l]), q_s, qi_s, w_s, layer=l)

        ffn2 = (vec(ffn2_norm_pre), vec(ffn2_norm_post), w2u, w2d)
        yp = _outproj_ffn(yp, cout_p, att_p, wo_c, wo_a, vec(mix_norm_post), True, *ffn2)
        flat = lambda a: a.reshape(1, bs * ts, a.shape[2])
        ys = _outproj_ffn(flat(ys), flat(cout_s), flat(att_s), wo_c, wo_a, vec(mix_norm_post), False,
                          *ffn2).reshape(bs, ts, d)

    def heads(stack, b, t):
        k, v, kidx, cstate = stack
        shape = (depth, b, t, N_HEADS, HEAD_DIM)
        return k.reshape(shape), v.reshape(shape), kidx, cstate

    return (yp, ys) + heads(stack_p, bp, tp) + heads(stack_s, bs, ts)
```
